```python
import jax, jax.numpy as jnp
from jax import lax
import numpy as np

D_MODEL = 1024
BATCH = 8
SEQ = 8192
DEPTH = 2

CHUNK = 64
N_A_LAYERS = max(1, DEPTH // 2)
N_B_LAYERS = DEPTH - N_A_LAYERS
CONV_WIDTH = 3
N_HEADS = 16
HEAD_DIM = D_MODEL // N_HEADS
Q_BLOCK = 128
PEER_HEADS = 8
PEER_N_KEYS = 128
PEER_N_EXPERTS = PEER_N_KEYS ** 2
PEER_D_KEY = 256
PEER_HALF = PEER_D_KEY // 2
PEER_TOPK = 16
PEER_TOKEN_BLOCK = 128
RMS_EPS = 1e-6

kernel_name = "yoco_shortconv_fox_peer_adaln"


def rmsnorm(x, gain):
    xf = x.astype(jnp.float32)
    y = xf * lax.rsqrt(jnp.mean(xf * xf, axis=-1, keepdims=True) + RMS_EPS)
    return (y * gain.astype(jnp.float32)).astype(x.dtype)


def ada_params(c, w, b, n):
    m = jax.nn.silu(c) @ w + b
    return [t[:, None, :] for t in jnp.split(m, n, axis=-1)]


def modulate(h, shift, scale):
    return h * (1 + scale) + shift


def short_conv_mixer(h, w_in, conv_w, w_out):
    b_gate, c_gate, u = jnp.split(h @ w_in, 3, axis=-1)
    z = lax.conv_general_dilated(
        c_gate * u, conv_w[:, None, :], window_strides=(1,),
        padding=[(CONV_WIDTH - 1, 0)], dimension_numbers=('NWC', 'WIO', 'NWC'),
        feature_group_count=D_MODEL)
    return (b_gate * z) @ w_out


def shared_kv(x, c, gain, ada_w, ada_b, w_kvf, f_bias):
    shift, scale = ada_params(c, ada_w, ada_b, 2)
    h = modulate(rmsnorm(x, gain), shift, scale)
    b_, s_ = x.shape[:2]
    k, v, f_logit = jnp.split(h @ w_kvf, [D_MODEL, 2 * D_MODEL], axis=-1)
    k = k.reshape(b_, s_, N_HEADS, HEAD_DIM).transpose(0, 2, 1, 3)
    v = v.reshape(b_, s_, N_HEADS, HEAD_DIM).transpose(0, 2, 1, 3)
    log_f = jax.nn.log_sigmoid((f_logit + f_bias).astype(jnp.float32))
    cum_f = lax.cumsum(log_f, axis=1).transpose(0, 2, 1)
    return k, v, cum_f


def forgetting_attention(h, w_q, w_o, k, v, cum_f):
    b_, s_ = h.shape[:2]
    q = (h @ w_q).reshape(b_, s_, N_HEADS, HEAD_DIM).transpose(0, 2, 1, 3)
    scale = HEAD_DIM ** -0.5
    outs = []
    for qs in range(0, s_, Q_BLOCK):
        ke = qs + Q_BLOCK
        logits = jnp.einsum('bhqd,bhkd->bhqk', q[:, :, qs:ke], k[:, :, :ke]).astype(jnp.float32) * scale
        logits = logits + cum_f[:, :, qs:ke, None] - cum_f[:, :, None, :ke]
        q_pos = qs + jnp.arange(Q_BLOCK)
        k_pos = jnp.arange(ke)
        logits = jnp.where(k_pos[None, :] <= q_pos[:, None], logits, -jnp.inf)
        p = jax.nn.softmax(logits, axis=-1).astype(v.dtype)
        outs.append(jnp.einsum('bhqk,bhkd->bhqd', p, v[:, :, :ke]))
    o = jnp.concatenate(outs, axis=2).transpose(0, 2, 1, 3).reshape(b_, s_, D_MODEL)
    return o @ w_o


def peer(h, w_q, sub_keys, u, v):
    b_, s_, d = h.shape
    t = b_ * s_
    ht = h.reshape(t, d)
    q = (ht @ w_q).reshape(t, PEER_HEADS, 2, PEER_HALF)
    sub_scores = jnp.einsum('thpd,hpkd->thpk', q, sub_keys).astype(jnp.float32)
    s_top, i_top = lax.top_k(sub_scores, PEER_TOPK)
    cand_s = (s_top[:, :, 0, :, None] + s_top[:, :, 1, None, :]).reshape(t, PEER_HEADS, PEER_TOPK * PEER_TOPK)
    cand_i = (i_top[:, :, 0, :, None] * PEER_N_KEYS + i_top[:, :, 1, None, :]).reshape(t, PEER_HEADS, PEER_TOPK * PEER_TOPK)
    best_s, best_pos = lax.top_k(cand_s, PEER_TOPK)
    expert_idx = jnp.take_along_axis(cand_i, best_pos, axis=-1)
    gates = jax.nn.softmax(best_s, axis=-1).astype(h.dtype)
    n_blk = t // PEER_TOKEN_BLOCK
    n_sel = PEER_HEADS * PEER_TOPK

    def block(args):
        xb, ib, gb = args
        ub = jnp.take(u, ib, axis=0)
        vb = jnp.take(v, ib, axis=0)
        act = jax.nn.gelu(jnp.einsum('td,ted->te', xb, ub))
        return jnp.einsum('te,ted->td', gb * act, vb)

    out = lax.map(block, (ht.reshape(n_blk, PEER_TOKEN_BLOCK, d),
                          expert_idx.reshape(n_blk, PEER_TOKEN_BLOCK, n_sel),
                          gates.reshape(n_blk, PEER_TOKEN_BLOCK, n_sel)))
    return out.reshape(b_, s_, d)


def setup_inputs(seed: int = 0) -> dict:
    key = jax.random.key(seed)
    ks = jax.random.split(key, 23)
    d = D_MODEL
    nrm = jax.random.normal
    inv = d ** -0.5
    return {
        "x": nrm(ks[0], (BATCH, SEQ, d), jnp.float32),
        "c": nrm(ks[1], (BATCH, d), jnp.float32),
        "mix_norm": 1.0 + 0.05 * nrm(ks[2], (DEPTH, d), jnp.float32),
        "mix_ada_w": 0.2 * inv * nrm(ks[3], (DEPTH, d, 3 * d), jnp.float32),
        "mix_ada_b": 0.02 * nrm(ks[4], (DEPTH, 3 * d), jnp.float32),
        "ffn_norm": 1.0 + 0.05 * nrm(ks[5], (DEPTH, d), jnp.float32),
        "ffn_ada_w": 0.2 * inv * nrm(ks[6], (DEPTH, d, 3 * d), jnp.float32),
        "ffn_ada_b": 0.02 * nrm(ks[7], (DEPTH, 3 * d), jnp.float32),
        "a_w_in": inv * nrm(ks[8], (N_A_LAYERS, d, 3 * d), jnp.float32),
        "a_conv": (CONV_WIDTH ** -0.5) * nrm(ks[9], (N_A_LAYERS, CONV_WIDTH, d), jnp.float32),
        "a_w_out": inv * nrm(ks[10], (N_A_LAYERS, d, d), jnp.float32),
        "kv_norm": 1.0 + 0.05 * nrm(ks[11], (d,), jnp.float32),
        "kv_ada_w": 0.2 * inv * nrm(ks[12], (d, 2 * d), jnp.float32),
        "kv_ada_b": 0.02 * nrm(ks[13], (2 * d,), jnp.float32),
        "kv_w": inv * nrm(ks[14], (d, 2 * d + N_HEADS), jnp.float32),
        "f_bias": jax.random.uniform(ks[15], (N_HEADS,), jnp.float32, 2.0, 7.0),
        "b_w_q": inv * nrm(ks[16], (N_B_LAYERS, d, d), jnp.float32),
        "b_w_o": inv * nrm(ks[17], (N_B_LAYERS, d, d), jnp.float32),
        "peer_w_q": inv * nrm(ks[18], (DEPTH, d, PEER_HEADS * PEER_D_KEY), jnp.float32),
        "peer_sub_keys": (PEER_HALF ** -0.5) * nrm(ks[19], (DEPTH, PEER_HEADS, 2, PEER_N_KEYS, PEER_HALF), jnp.float32),
        "peer_u": inv * nrm(ks[20], (DEPTH, PEER_N_EXPERTS, d), jnp.float32),
        "peer_v": (PEER_HEADS ** -0.5) * nrm(ks[21], (DEPTH, PEER_N_EXPERTS, d), jnp.float32),
        "final_norm": 1.0 + 0.05 * nrm(ks[22], (d,), jnp.float32),
    }


def reference(x, c, mix_norm, mix_ada_w, mix_ada_b, ffn_norm, ffn_ada_w, ffn_ada_b,
              a_w_in, a_conv, a_w_out, kv_norm, kv_ada_w, kv_ada_b, kv_w, f_bias,
              b_w_q, b_w_o, peer_w_q, peer_sub_keys, peer_u, peer_v, final_norm):
    kv = None
    for layer in range(DEPTH):
        shift, scale, gate = ada_params(c, mix_ada_w[layer], mix_ada_b[layer], 3)
        h = modulate(rmsnorm(x, mix_norm[layer]), shift, scale)
        if layer < N_A_LAYERS:
            y = short_conv_mixer(h, a_w_in[layer], a_conv[layer], a_w_out[layer])
        else:
            j = layer - N_A_LAYERS
            y = forgetting_attention(h, b_w_q[j], b_w_o[j], kv[0], kv[1], kv[2])
        x = x + (1 + gate) * y
        shift, scale, gate = ada_params(c, ffn_ada_w[layer], ffn_ada_b[layer], 3)
        h = modulate(rmsnorm(x, ffn_norm[layer]), shift, scale)
        x = x + (1 + gate) * peer(h, peer_w_q[layer], peer_sub_keys[layer], peer_u[layer], peer_v[layer])
        if layer == N_A_LAYERS - 1:
            kv = shared_kv(x, c, kv_norm, kv_ada_w, kv_ada_b, kv_w, f_bias)
    return rmsnorm(x, final_norm)
```

```python
import functools

import jax
import jax.numpy as jnp
from jax import lax
from jax.experimental import pallas as pl
from jax.experimental.pallas import tpu as pltpu

D_MODEL = 1024
N_HEADS = 16
HEAD_DIM = 64
PEER_HEADS = 8
PEER_N_KEYS = 128
PEER_TOPK = 16
N_SEL = PEER_HEADS * PEER_TOPK
RMS_EPS = 1e-6

LANES = 128
SUBLANES = 8
HALF_D = D_MODEL // 2
ROWS_PER_EXPERT = HALF_D // LANES
GATHER_STRIDE = 136
VMEM_LIMIT = 56 * 1024 * 1024

BF16 = jnp.bfloat16
F32 = jnp.float32
NT_DIMS = (((1,), (1,)), ((), ()))


def _cparams(*sem):
    return pltpu.CompilerParams(dimension_semantics=sem, vmem_limit_bytes=VMEM_LIMIT)


def _norm_mod(x, gain, shift, scale):
    r = lax.rsqrt(jnp.mean(x * x, axis=-1, keepdims=True) + RMS_EPS)
    return (x * r) * gain * (1.0 + scale) + shift


def _ada_kernel(c_ref, w_ref, b_ref, o_ref):
    c = c_ref[...]
    sc = c * jax.nn.sigmoid(c)
    o_ref[...] = jnp.dot(sc, w_ref[...], precision=lax.Precision.HIGHEST,
                         preferred_element_type=F32) + b_ref[...]


def ada_linear(c, w, b, *, tn=1024):
    bsz, d = c.shape
    n = w.shape[1]
    return pl.pallas_call(
        _ada_kernel,
        grid=(n // tn,),
        in_specs=[pl.BlockSpec((bsz, d), lambda j: (0, 0)),
                  pl.BlockSpec((d, tn), lambda j: (0, j)),
                  pl.BlockSpec((1, tn), lambda j: (0, j))],
        out_specs=pl.BlockSpec((bsz, tn), lambda j: (0, j)),
        out_shape=jax.ShapeDtypeStruct((bsz, n), F32),
        compiler_params=_cparams("arbitrary"),
        name="ada_linear",
    )(c, w, b.reshape(1, n))


def _mixer_a_kernel(x_ref, shift_ref, scale_ref, gate1_ref, gain_ref, win_ref, conv_ref, wout_ref,
                    o_ref, cu_ref, *, ts):
    d = D_MODEL
    x = x_ref[0]
    h = _norm_mod(x, gain_ref[...], shift_ref[0], scale_ref[0])
    proj = jnp.dot(h.astype(BF16), win_ref[...], preferred_element_type=F32)
    b_gate = proj[:, :d]
    cu = proj[:, d:2 * d] * proj[:, 2 * d:]

    @pl.when(pl.program_id(1) == 0)
    def _():
        cu_ref[0:SUBLANES, :] = jnp.zeros((SUBLANES, d), F32)

    cu_ref[SUBLANES:SUBLANES + ts, :] = cu
    w = conv_ref[...]
    z = (w[0:1] * cu_ref[SUBLANES - 2:SUBLANES - 2 + ts, :]
         + w[1:2] * cu_ref[SUBLANES - 1:SUBLANES - 1 + ts, :]
         + w[2:3] * cu)
    cu_ref[0:SUBLANES, :] = cu_ref[ts:ts + SUBLANES, :]
    y = jnp.dot((b_gate * z).astype(BF16), wout_ref[...], preferred_element_type=F32)
    o_ref[0] = x + gate1_ref[0] * y


def mixer_a(x, shift, scale, gate1, gain, w_in, conv_w, w_out, *, ts=256):
    bsz, s, d = x.shape
    vec = pl.BlockSpec((1, 1, d), lambda b, i: (b, 0, 0))
    return pl.pallas_call(
        functools.partial(_mixer_a_kernel, ts=ts),
        grid=(bsz, s // ts),
        in_specs=[pl.BlockSpec((1, ts, d), lambda b, i: (b, i, 0)), vec, vec, vec,
                  pl.BlockSpec((1, d), lambda b, i: (0, 0)),
                  pl.BlockSpec((d, 3 * d), lambda b, i: (0, 0)),
                  pl.BlockSpec((3, d), lambda b, i: (0, 0)),
                  pl.BlockSpec((d, d), lambda b, i: (0, 0))],
        out_specs=pl.BlockSpec((1, ts, d), lambda b, i: (b, i, 0)),
        out_shape=jax.ShapeDtypeStruct((bsz, s, d), F32),
        scratch_shapes=[pltpu.VMEM((ts + 2 * SUBLANES, d), F32)],
        compiler_params=_cparams("arbitrary", "arbitrary"),
        name="mixer_a",
    )(x, shift, scale, gate1, gain, w_in, conv_w, w_out)


def _topk_rows(s, k, payload=None):
    n = s.shape[0]
    rows = lax.broadcasted_iota(jnp.int32, s.shape, 0)
    vals, idxs, pays = [], [], []
    for _ in range(k):
        m = jnp.max(s, axis=0, keepdims=True)
        am = jnp.min(jnp.where(s == m, rows, n), axis=0, keepdims=True)
        hit = rows == am
        vals.append(m)
        idxs.append(am)
        if payload is not None:
            pays.append(jnp.sum(jnp.where(hit, payload, 0), axis=0, keepdims=True))
        s = jnp.where(hit, -jnp.inf, s)
    return vals, idxs, pays


def _router_kernel(x_ref, shift_ref, scale_ref, gain_ref, wq_ref, keys_ref, h_ref, idx_ref, gates_ref):
    x = x_ref[0]
    h = _norm_mod(x, gain_ref[...], shift_ref[0], scale_ref[0])
    hb = h.astype(BF16)
    h_ref[0] = hb
    q = jnp.dot(hb, wq_ref[...], preferred_element_type=F32).astype(BF16)
    idx_rows, gate_rows = [], []
    for head in range(PEER_HEADS):
        top_s, top_i = [], []
        for part in range(2):
            hp = head * 2 + part
            st = lax.dot_general(keys_ref[hp], q[:, hp * LANES:(hp + 1) * LANES], NT_DIMS,
                                 preferred_element_type=F32)
            vals, idxs, _ = _topk_rows(st, PEER_TOPK)
            top_s.append(vals)
            top_i.append(idxs)
        s2 = jnp.concatenate(top_s[1], axis=0)
        i2 = jnp.concatenate(top_i[1], axis=0)
        cand_s = jnp.concatenate([top_s[0][a] + s2 for a in range(PEER_TOPK)], axis=0)
        cand_i = jnp.concatenate([top_i[0][a] * PEER_N_KEYS + i2 for a in range(PEER_TOPK)], axis=0)
        best_s, _, best_e = _topk_rows(cand_s, PEER_TOPK, payload=cand_i)
        bs = jnp.concatenate(best_s, axis=0)
        e = jnp.exp(bs - bs[0:1])
        gate_rows.append(e / jnp.sum(e, axis=0, keepdims=True))
        idx_rows.append(jnp.concatenate(best_e, axis=0))
    idx_ref[0] = (jnp.concatenate(idx_rows, axis=0) * ROWS_PER_EXPERT).T
    gates_ref[0] = jnp.concatenate(gate_rows, axis=0).T


def peer_router(x, shift, scale, gain, w_q, keys, *, tb=256):
    bsz, s, d = x.shape
    vec = pl.BlockSpec((1, 1, d), lambda b, i: (b, 0, 0))
    tok = lambda width: pl.BlockSpec((1, tb, width), lambda b, i: (b, i, 0))
    return pl.pallas_call(
        _router_kernel,
        grid=(bsz, s // tb),
        in_specs=[tok(d), vec, vec,
                  pl.BlockSpec((1, d), lambda b, i: (0, 0)),
                  pl.BlockSpec(w_q.shape, lambda b, i: (0, 0)),
                  pl.BlockSpec(keys.shape, lambda b, i: (0, 0, 0))],
        out_specs=[tok(d), tok(N_SEL), tok(N_SEL)],
        out_shape=[jax.ShapeDtypeStruct((bsz, s, d), BF16),
                   jax.ShapeDtypeStruct((bsz, s, N_SEL), jnp.int32),
                   jax.ShapeDtypeStruct((bsz, s, N_SEL), F32)],
        compiler_params=_cparams("arbitrary", "arbitrary"),
        name="peer_router",
    )(x, shift, scale, gain, w_q, keys)


def pack_table(tab):
    bits = lax.bitcast_convert_type(tab.astype(BF16), jnp.uint16).astype(jnp.uint32)
    words = bits[:, :HALF_D] | (bits[:, HALF_D:] << 16)
    return lax.bitcast_convert_type(words, jnp.int32).reshape(-1, LANES)


def _unpack(words):
    lo = pltpu.bitcast(words << 16, F32)
    hi = pltpu.bitcast(words & jnp.int32(-65536), F32)
    return lo, hi


def _peer_u_kernel(idx_ref, h_ref, gate_ref, tab_ref, w_ref, g_ref, *, tb):
    def group(g, carry):
        r0 = pl.multiple_of(g * SUBLANES, SUBLANES)
        x8 = h_ref[pl.ds(r0, SUBLANES), :]
        row = lax.broadcasted_iota(jnp.int32, (SUBLANES, N_SEL), 0)
        acc = jnp.zeros((SUBLANES, N_SEL), F32)
        for j in range(SUBLANES):
            base = (r0 + j) * N_SEL
            for k in range(N_SEL):
                e4 = pl.multiple_of(idx_ref[base + k], ROWS_PER_EXPERT)
                g_ref[j, pl.ds(k, ROWS_PER_EXPERT, stride=GATHER_STRIDE), :] = (
                    tab_ref[pl.ds(e4, ROWS_PER_EXPERT), :])
            res = jnp.zeros((SUBLANES, N_SEL), F32)
            for c in range(ROWS_PER_EXPERT):
                lo, hi = _unpack(g_ref[j, pl.ds(c * GATHER_STRIDE, N_SEL), :])
                res += lax.dot_general(x8[:, c * LANES:(c + 1) * LANES], lo.astype(BF16), NT_DIMS,
                                       preferred_element_type=F32)
                res += lax.dot_general(x8[:, HALF_D + c * LANES:HALF_D + (c + 1) * LANES],
                                       hi.astype(BF16), NT_DIMS, preferred_element_type=F32)
            acc = jnp.where(row == j, res, acc)
        w_ref[pl.ds(r0, SUBLANES), :] = gate_ref[pl.ds(r0, SUBLANES), :] * jax.nn.gelu(acc)
        return carry

    lax.fori_loop(0, tb // SUBLANES, group, 0)


def peer_u_phase(idx_flat, h, gates, utab, *, tb=64):
    t = h.shape[0]
    return pl.pallas_call(
        functools.partial(_peer_u_kernel, tb=tb),
        grid=(t // tb,),
        in_specs=[pl.BlockSpec((tb * N_SEL,), lambda i: (i,), memory_space=pltpu.SMEM),
                  pl.BlockSpec((tb, D_MODEL), lambda i: (i, 0)),
                  pl.BlockSpec((tb, N_SEL), lambda i: (i, 0)),
                  pl.BlockSpec(memory_space=pltpu.VMEM)],
        out_specs=pl.BlockSpec((tb, N_SEL), lambda i: (i, 0)),
        out_shape=jax.ShapeDtypeStruct((t, N_SEL), F32),
        scratch_shapes=[pltpu.VMEM((SUBLANES, ROWS_PER_EXPERT * GATHER_STRIDE, LANES), jnp.int32)],
        compiler_params=_cparams("arbitrary"),
        name="peer_u",
    )(idx_flat, h, gates, utab)


def _peer_v_kernel(idx_ref, w_ref, x_ref, gate1_ref, tab_ref, o_ref, s_ref, *, tb):
    def group(g, carry):
        r0 = pl.multiple_of(g * SUBLANES, SUBLANES)
        for j in range(SUBLANES):
            base = (r0 + j) * N_SEL
            acc_lo = jnp.zeros((ROWS_PER_EXPERT, LANES), F32)
            acc_hi = jnp.zeros((ROWS_PER_EXPERT, LANES), F32)
            for k in range(N_SEL):
                e4 = pl.multiple_of(idx_ref[base + k], ROWS_PER_EXPERT)
                lo, hi = _unpack(tab_ref[pl.ds(e4, ROWS_PER_EXPERT), :])
                wk = w_ref[base + k]
                acc_lo = acc_lo + wk * lo
                acc_hi = acc_hi + wk * hi
            s_ref[j * SUBLANES:j * SUBLANES + ROWS_PER_EXPERT, :] = acc_lo
            s_ref[j * SUBLANES + ROWS_PER_EXPERT:(j + 1) * SUBLANES, :] = acc_hi
        for c in range(SUBLANES):
            y = s_ref[pl.ds(c, SUBLANES, stride=SUBLANES), :]
            cols = slice(c * LANES, (c + 1) * LANES)
            o_ref[pl.ds(r0, SUBLANES), cols] = x_ref[pl.ds(r0, SUBLANES), cols] + gate1_ref[0, :, cols] * y
        return carry

    lax.fori_loop(0, tb // SUBLANES, group, 0)


def peer_v_phase(idx_flat, w_flat, x, gate1, vtab, *, tb=64):
    t = x.shape[0]
    blocks_per_batch = (t // gate1.shape[0]) // tb
    return pl.pallas_call(
        functools.partial(_peer_v_kernel, tb=tb),
        grid=(t // tb,),
        in_specs=[pl.BlockSpec((tb * N_SEL,), lambda i: (i,), memory_space=pltpu.SMEM),
                  pl.BlockSpec((tb * N_SEL,), lambda i: (i,), memory_space=pltpu.SMEM),
                  pl.BlockSpec((tb, D_MODEL), lambda i: (i, 0)),
                  pl.BlockSpec((1, 1, D_MODEL), lambda i: (i // blocks_per_batch, 0, 0)),
                  pl.BlockSpec(memory_space=pltpu.VMEM)],
        out_specs=pl.BlockSpec((tb, D_MODEL), lambda i: (i, 0)),
        out_shape=jax.ShapeDtypeStruct((t, D_MODEL), F32),
        scratch_shapes=[pltpu.VMEM((SUBLANES * SUBLANES, LANES), F32)],
        compiler_params=_cparams("arbitrary"),
        name="peer_v",
    )(idx_flat, w_flat, x, gate1, vtab)


def peer_layer(x, shift, scale, gate1, gain, w_q, keys, utab, vtab):
    bsz, s, d = x.shape
    t = bsz * s
    h, idx, gates = peer_router(x, shift, scale, gain, w_q, keys)
    idx_flat = idx.reshape(t * N_SEL)
    w = peer_u_phase(idx_flat, h.reshape(t, d), gates.reshape(t, N_SEL), utab)
    out = peer_v_phase(idx_flat, w.reshape(t * N_SEL), x.reshape(t, d), gate1, vtab)
    return out.reshape(bsz, s, d)


def _split3(x):
    a = x.astype(BF16)
    r = x - a.astype(F32)
    b = r.astype(BF16)
    c = (r - b.astype(F32)).astype(BF16)
    return a, b, c


def _kvq_kernel(x_ref, kshift_ref, kscale_ref, kgain_ref, qshift_ref, qscale_ref, qgain_ref,
                wkv_ref, wf_hi_ref, wf_lo_ref, fb_ref, wq_ref,
                q_ref, k_ref, v_ref, cf_ref, carry_ref, *, ts):
    d = D_MODEL
    x = x_ref[0]
    r = lax.rsqrt(jnp.mean(x * x, axis=-1, keepdims=True) + RMS_EPS)
    xn = x * r
    hk = xn * kgain_ref[...] * (1.0 + kscale_ref[0]) + kshift_ref[0]
    hq = xn * qgain_ref[...] * (1.0 + qscale_ref[0]) + qshift_ref[0]
    hk_hi = hk.astype(BF16)
    kv = jnp.dot(hk_hi, wkv_ref[...], preferred_element_type=F32)
    k_ref[0] = kv[:, :d].astype(BF16)
    v_ref[0] = kv[:, d:].astype(BF16)
    q = jnp.dot(hq.astype(BF16), wq_ref[...], preferred_element_type=F32)
    q_ref[0] = (q * (HEAD_DIM ** -0.5)).astype(BF16)

    hk_lo = (hk - hk_hi.astype(F32)).astype(BF16)
    f_logit = (jnp.dot(hk_hi, wf_hi_ref[...], preferred_element_type=F32)
               + jnp.dot(hk_lo, wf_hi_ref[...], preferred_element_type=F32)
               + jnp.dot(hk_hi, wf_lo_ref[...], preferred_element_type=F32))
    z = f_logit + fb_ref[...]
    log_f = jnp.minimum(z, 0.0) - jnp.log1p(jnp.exp(-jnp.abs(z)))

    @pl.when(pl.program_id(1) == 0)
    def _():
        carry_ref[...] = jnp.zeros_like(carry_ref)

    tri = (lax.broadcasted_iota(jnp.int32, (ts, ts), 0)
           >= lax.broadcasted_iota(jnp.int32, (ts, ts), 1)).astype(BF16)
    cum = carry_ref[...]
    for piece in _split3(log_f):
        cum = cum + jnp.dot(tri, piece, preferred_element_type=F32)
    cf_ref[0] = cum
    carry_ref[...] = cum[ts - 1:ts, :]


def kvq_proj(x, kshift, kscale, kgain, qshift, qscale, qgain, w_kv, wf_hi, wf_lo, f_bias, w_q, *, ts=512):
    bsz, s, d = x.shape
    vec = pl.BlockSpec((1, 1, d), lambda b, i: (b, 0, 0))
    const = lambda a: pl.BlockSpec(a.shape, lambda b, i: (0,) * a.ndim)
    tok = lambda width: pl.BlockSpec((1, ts, width), lambda b, i: (b, i, 0))
    return pl.pallas_call(
        functools.partial(_kvq_kernel, ts=ts),
        grid=(bsz, s // ts),
        in_specs=[tok(d), vec, vec, const(kgain), vec, vec, const(qgain),
                  const(w_kv), const(wf_hi), const(wf_lo), const(f_bias), const(w_q)],
        out_specs=[tok(d), tok(d), tok(d), tok(LANES)],
        out_shape=[jax.ShapeDtypeStruct((bsz, s, d), BF16)] * 3
        + [jax.ShapeDtypeStruct((bsz, s, LANES), F32)],
        scratch_shapes=[pltpu.VMEM((1, LANES), F32)],
        compiler_params=_cparams("arbitrary", "arbitrary"),
        name="kvq_proj",
    )(x, kshift, kscale, kgain, qshift, qscale, qgain, w_kv, wf_hi, wf_lo, f_bias, w_q)


def _fox_kernel(q_ref, k_ref, v_ref, cfq_ref, cfk_ref, o_ref, acc_ref, m_ref, l_ref, *, tq, tk):
    qi = pl.program_id(1)
    ki = pl.program_id(2)

    @pl.when(ki == 0)
    def _():
        acc_ref[...] = jnp.zeros_like(acc_ref)
        m_ref[...] = jnp.full_like(m_ref, -jnp.inf)
        l_ref[...] = jnp.zeros_like(l_ref)

    @pl.when(ki <= qi)
    def _():
        lane = lax.broadcasted_iota(jnp.int32, (tq, LANES), 1)
        first_head = lane < HEAD_DIM
        causal = ((lax.broadcasted_iota(jnp.int32, (tq, tk), 1) + ki * tk)
                  <= (lax.broadcasted_iota(jnp.int32, (tq, tk), 0) + qi * tq))
        cfq = cfq_ref[0]
        cfk = cfk_ref[0]
        for pair in range(N_HEADS // 2):
            cols = slice(pair * LANES, (pair + 1) * LANES)
            qp = q_ref[0, :, cols]
            kp = k_ref[0, :, cols]
            vp = v_ref[0, :, cols]
            upd = []
            for sub in range(2):
                hd = pair * 2 + sub
                mine = first_head if sub == 0 else jnp.logical_not(first_head)
                qm = jnp.where(mine, qp, jnp.zeros_like(qp))
                s = lax.dot_general(qm, kp, NT_DIMS, preferred_element_type=F32)
                s = s + cfq[:, hd:hd + 1] - cfk[hd:hd + 1, :]
                s = jnp.where(causal, s, -jnp.inf)
                m_prev = m_ref[hd]
                m_new = jnp.maximum(m_prev, jnp.max(s, axis=1, keepdims=True))
                alpha = jnp.exp(m_prev - m_new)
                p = jnp.exp(s - m_new[:, 0:1])
                l_ref[hd] = alpha * l_ref[hd] + jnp.sum(p, axis=1, keepdims=True)
                m_ref[hd] = m_new
                pv = jnp.dot(p.astype(BF16), vp, preferred_element_type=F32)
                upd.append(alpha * acc_ref[:, cols] + pv)
            acc_ref[:, cols] = jnp.where(first_head, upd[0], upd[1])

    @pl.when(ki == qi)
    def _():
        lane = lax.broadcasted_iota(jnp.int32, (tq, LANES), 1)
        first_head = lane < HEAD_DIM
        for pair in range(N_HEADS // 2):
            cols = slice(pair * LANES, (pair + 1) * LANES)
            l = jnp.where(first_head, l_ref[2 * pair], l_ref[2 * pair + 1])
            o_ref[0, :, cols] = (acc_ref[:, cols] / l).astype(o_ref.dtype)


def fox_attention(q, k, v, cf_tok, cf_head, *, tq=256, tk=256):
    bsz, s, d = q.shape
    return pl.pallas_call(
        functools.partial(_fox_kernel, tq=tq, tk=tk),
        grid=(bsz, s // tq, s // tk),
        in_specs=[pl.BlockSpec((1, tq, d), lambda b, i, j: (b, i, 0)),
                  pl.BlockSpec((1, tk, d), lambda b, i, j: (b, jnp.minimum(i, j), 0)),
                  pl.BlockSpec((1, tk, d), lambda b, i, j: (b, jnp.minimum(i, j), 0)),
                  pl.BlockSpec((1, tq, LANES), lambda b, i, j: (b, i, 0)),
                  pl.BlockSpec((1, N_HEADS, tk), lambda b, i, j: (b, 0, jnp.minimum(i, j)))],
        out_specs=pl.BlockSpec((1, tq, d), lambda b, i, j: (b, i, 0)),
        out_shape=jax.ShapeDtypeStruct((bsz, s, d), BF16),
        scratch_shapes=[pltpu.VMEM((tq, d), F32),
                        pltpu.VMEM((N_HEADS, tq, LANES), F32),
                        pltpu.VMEM((N_HEADS, tq, LANES), F32)],
        compiler_params=_cparams("arbitrary", "arbitrary", "arbitrary"),
        name="fox_attention",
    )(q, k, v, cf_tok, cf_head)


def _out_proj_kernel(a_ref, x_ref, gate1_ref, w_ref, o_ref):
    y = jnp.dot(a_ref[0], w_ref[...], preferred_element_type=F32)
    o_ref[0] = x_ref[0] + gate1_ref[0] * y


def out_proj(a, x, gate1, w, *, ts=512):
    bsz, s, d = x.shape
    tok = pl.BlockSpec((1, ts, d), lambda b, i: (b, i, 0))
    return pl.pallas_call(
        _out_proj_kernel,
        grid=(bsz, s // ts),
        in_specs=[tok, tok, pl.BlockSpec((1, 1, d), lambda b, i: (b, 0, 0)),
                  pl.BlockSpec((d, d), lambda b, i: (0, 0))],
        out_specs=tok,
        out_shape=jax.ShapeDtypeStruct((bsz, s, d), F32),
        compiler_params=_cparams("arbitrary", "arbitrary"),
        name="out_proj",
    )(a, x, gate1, w)


def _final_norm_kernel(x_ref, gain_ref, o_ref):
    x = x_ref[...]
    r = lax.rsqrt(jnp.mean(x * x, axis=-1, keepdims=True) + RMS_EPS)
    o_ref[...] = (x * r) * gain_ref[...]


def final_norm_call(x, gain, *, ts=1024):
    t, d = x.shape
    return pl.pallas_call(
        _final_norm_kernel,
        grid=(t // ts,),
        in_specs=[pl.BlockSpec((ts, d), lambda i: (i, 0)), pl.BlockSpec((1, d), lambda i: (0, 0))],
        out_specs=pl.BlockSpec((ts, d), lambda i: (i, 0)),
        out_shape=jax.ShapeDtypeStruct((t, d), F32),
        compiler_params=_cparams("arbitrary"),
        name="final_norm",
    )(x, gain)


def kernel(x, c, mix_norm, mix_ada_w, mix_ada_b, ffn_norm, ffn_ada_w, ffn_ada_b, a_w_in, a_conv, a_w_out, kv_norm, kv_ada_w, kv_ada_b, kv_w, f_bias, b_w_q, b_w_o, peer_w_q, peer_sub_keys, peer_u, peer_v, final_norm):
    bsz, s, d = x.shape
    depth = mix_norm.shape[0]
    n_a = a_w_in.shape[0]

    ada_w = jnp.concatenate([mix_ada_w[l] for l in range(depth)] + [ffn_ada_w[l] for l in range(depth)]
                            + [kv_ada_w], axis=1)
    ada_b = jnp.concatenate([mix_ada_b[l] for l in range(depth)] + [ffn_ada_b[l] for l in range(depth)]
                            + [kv_ada_b], axis=0)
    ada = ada_linear(c, ada_w, ada_b)

    def ada_chunk(i):
        return ada[:, i * d:(i + 1) * d].reshape(bsz, 1, d)

    def mix_ada(l):
        return ada_chunk(3 * l), ada_chunk(3 * l + 1), 1.0 + ada_chunk(3 * l + 2)

    def ffn_ada(l):
        o = 3 * depth + 3 * l
        return ada_chunk(o), ada_chunk(o + 1), 1.0 + ada_chunk(o + 2)

    kv_shift, kv_scale = ada_chunk(6 * depth), ada_chunk(6 * depth + 1)

    kv = None
    for layer in range(depth):
        shift, scale, gate1 = mix_ada(layer)
        gain = mix_norm[layer].reshape(1, d)
        if layer < n_a:
            x = mixer_a(x, shift, scale, gate1, gain, a_w_in[layer].astype(BF16), a_conv[layer],
                        a_w_out[layer].astype(BF16))
        else:
            j = layer - n_a
            if j == 0:
                w_f = jnp.pad(kv_w[:, 2 * d:], ((0, 0), (0, LANES - N_HEADS)))
                wf_hi = w_f.astype(BF16)
                wf_lo = (w_f - wf_hi.astype(F32)).astype(BF16)
                fb = jnp.pad(f_bias, (0, LANES - N_HEADS)).reshape(1, LANES)
                q, k, v, cf = kvq_proj(x, kv_shift, kv_scale, kv_norm.reshape(1, d), shift, scale, gain,
                                       kv_w[:, :2 * d].astype(BF16), wf_hi, wf_lo, fb, b_w_q[j].astype(BF16))
                kv = (k, v, cf, jnp.swapaxes(cf[:, :, :N_HEADS], 1, 2))
            else:
                raise NotImplementedError("more than one attention layer")
            attn = fox_attention(q, kv[0], kv[1], kv[2], kv[3])
            x = out_proj(attn, x, gate1, b_w_o[j].astype(BF16))
        shift, scale, gate1 = ffn_ada(layer)
        keys = peer_sub_keys[layer].reshape(2 * PEER_HEADS, PEER_N_KEYS, -1).astype(BF16)
        x = peer_layer(x, shift, scale, gate1, ffn_norm[layer].reshape(1, d), peer_w_q[layer].astype(BF16),
                       keys, pack_table(peer_u[layer]), pack_table(peer_v[layer]))
    return final_norm_call(x.reshape(bsz * s, d), final_norm.reshape(1, d)).reshape(bsz, s, d)
```

```python
import functools

import jax
import jax.numpy as jnp
from jax import lax
from jax.experimental import pallas as pl
from jax.experimental.pallas import tpu as pltpu

D_MODEL = 1024
N_HEADS = 16
HEAD_DIM = 64
PEER_HEADS = 8
PEER_N_KEYS = 128
PEER_TOPK = 16
N_SEL = PEER_HEADS * PEER_TOPK
RMS_EPS = 1e-6
LOG2_E = 1.4426950408889634

LANES = 128
SUBLANES = 8
HALF_D = D_MODEL // 2
ROWS_PER_EXPERT = HALF_D // LANES
SUBGROUPS = 4
GATHER_STRIDE = 136
VMEM_LIMIT = 56 * 1024 * 1024

BF16 = jnp.bfloat16
F32 = jnp.float32
NT_DIMS = (((1,), (1,)), ((), ()))


def _cparams(*sem):
    return pltpu.CompilerParams(dimension_semantics=sem, vmem_limit_bytes=VMEM_LIMIT)


def _norm_mod(x, gain, shift, scale):
    r = lax.rsqrt(jnp.mean(x * x, axis=-1, keepdims=True) + RMS_EPS)
    return (x * r) * gain * (1.0 + scale) + shift


def _ada_kernel(c_ref, w_ref, b_ref, o_ref):
    c = c_ref[...]
    sc = c * jax.nn.sigmoid(c)
    o_ref[...] = jnp.dot(sc, w_ref[...], precision=lax.Precision.HIGHEST,
                         preferred_element_type=F32) + b_ref[...]


def ada_linear(c, w, b, *, tn=1024):
    bsz, d = c.shape
    n = w.shape[1]
    return pl.pallas_call(
        _ada_kernel,
        grid=(n // tn,),
        in_specs=[pl.BlockSpec((bsz, d), lambda j: (0, 0)),
                  pl.BlockSpec((d, tn), lambda j: (0, j)),
                  pl.BlockSpec((1, tn), lambda j: (0, j))],
        out_specs=pl.BlockSpec((bsz, tn), lambda j: (0, j)),
        out_shape=jax.ShapeDtypeStruct((bsz, n), F32),
        compiler_params=_cparams("arbitrary"),
        name="ada_linear",
    )(c, w, b.reshape(1, n))


def _mixer_a_kernel(x_ref, shift_ref, scale_ref, gate1_ref, gain_ref, win_ref, conv_ref, wout_ref,
                    o_ref, cu_ref, *, ts):
    d = D_MODEL
    x = x_ref[0]
    h = _norm_mod(x, gain_ref[...], shift_ref[0], scale_ref[0])
    proj = jnp.dot(h.astype(BF16), win_ref[...], preferred_element_type=F32)
    b_gate = proj[:, :d]
    cu = proj[:, d:2 * d] * proj[:, 2 * d:]

    @pl.when(pl.program_id(1) == 0)
    def _():
        cu_ref[0:SUBLANES, :] = jnp.zeros((SUBLANES, d), F32)

    cu_ref[SUBLANES:SUBLANES + ts, :] = cu
    w = conv_ref[...]
    z = (w[0:1] * cu_ref[SUBLANES - 2:SUBLANES - 2 + ts, :]
         + w[1:2] * cu_ref[SUBLANES - 1:SUBLANES - 1 + ts, :]
         + w[2:3] * cu)
    cu_ref[0:SUBLANES, :] = cu_ref[ts:ts + SUBLANES, :]
    y = jnp.dot((b_gate * z).astype(BF16), wout_ref[...], preferred_element_type=F32)
    o_ref[0] = x + gate1_ref[0] * y


def mixer_a(x, shift, scale, gate1, gain, w_in, conv_w, w_out, *, ts=256):
    bsz, s, d = x.shape
    vec = pl.BlockSpec((1, 1, d), lambda b, i: (b, 0, 0))
    return pl.pallas_call(
        functools.partial(_mixer_a_kernel, ts=ts),
        grid=(bsz, s // ts),
        in_specs=[pl.BlockSpec((1, ts, d), lambda b, i: (b, i, 0)), vec, vec, vec,
                  pl.BlockSpec((1, d), lambda b, i: (0, 0)),
                  pl.BlockSpec((d, 3 * d), lambda b, i: (0, 0)),
                  pl.BlockSpec((3, d), lambda b, i: (0, 0)),
                  pl.BlockSpec((d, d), lambda b, i: (0, 0))],
        out_specs=pl.BlockSpec((1, ts, d), lambda b, i: (b, i, 0)),
        out_shape=jax.ShapeDtypeStruct((bsz, s, d), F32),
        scratch_shapes=[pltpu.VMEM((ts + 2 * SUBLANES, d), F32)],
        compiler_params=_cparams("arbitrary", "arbitrary"),
        name="mixer_a",
    )(x, shift, scale, gate1, gain, w_in, conv_w, w_out)


def _topk_rows(s, k, payload=None):
    n = s.shape[0]
    rows = lax.broadcasted_iota(jnp.int32, s.shape, 0).astype(F32)
    vals, idxs, pays = [], [], []
    for _ in range(k):
        m = jnp.max(s, axis=0, keepdims=True)
        am = jnp.min(jnp.where(s == m, rows, float(n)), axis=0, keepdims=True)
        hit = rows == am
        vals.append(m)
        idxs.append(am)
        if payload is not None:
            pays.append(jnp.sum(jnp.where(hit, payload, 0), axis=0, keepdims=True))
        s = jnp.where(hit, -jnp.inf, s)
    return vals, idxs, pays


def _router_kernel(x_ref, shift_ref, scale_ref, gain_ref, wq_ref, keys_ref, h_ref, idx_ref, gates_ref):
    x = x_ref[0]
    h = _norm_mod(x, gain_ref[...], shift_ref[0], scale_ref[0])
    hb = h.astype(BF16)
    h_ref[0] = hb
    q = jnp.dot(hb, wq_ref[...], preferred_element_type=F32).astype(BF16)
    idx_rows, gate_rows = [], []
    for head in range(PEER_HEADS):
        top_s, top_i = [], []
        for part in range(2):
            hp = head * 2 + part
            st = lax.dot_general(keys_ref[hp], q[:, hp * LANES:(hp + 1) * LANES], NT_DIMS,
                                 preferred_element_type=F32)
            vals, idxs, _ = _topk_rows(st, PEER_TOPK)
            top_s.append(vals)
            top_i.append(idxs)
        s2 = jnp.concatenate(top_s[1], axis=0)
        i2 = jnp.concatenate(top_i[1], axis=0).astype(jnp.int32)
        n_b = [PEER_TOPK // (a + 1) for a in range(PEER_TOPK)]
        pad = -(-sum(n_b) // SUBLANES) * SUBLANES - sum(n_b)
        cand_s = jnp.concatenate([top_s[0][a] + s2[:n_b[a]] for a in range(PEER_TOPK)]
                                 + [jnp.full((pad, s2.shape[1]), -jnp.inf, F32)], axis=0)
        cand_i = jnp.concatenate([top_i[0][a].astype(jnp.int32) * PEER_N_KEYS + i2[:n_b[a]]
                                  for a in range(PEER_TOPK)]
                                 + [jnp.zeros((pad, s2.shape[1]), jnp.int32)], axis=0)
        best_s, _, best_e = _topk_rows(cand_s, PEER_TOPK, payload=cand_i)
        bs = jnp.concatenate(best_s, axis=0)
        e = jnp.exp(bs - bs[0:1])
        gate_rows.append(e / jnp.sum(e, axis=0, keepdims=True))
        idx_rows.append(jnp.concatenate(best_e, axis=0))
    idx_ref[0] = (jnp.concatenate(idx_rows, axis=0) * ROWS_PER_EXPERT).T
    gates_ref[0] = jnp.concatenate(gate_rows, axis=0).T


def peer_router(x, shift, scale, gain, w_q, keys, *, tb=256):
    bsz, s, d = x.shape
    vec = pl.BlockSpec((1, 1, d), lambda b, i: (b, 0, 0))
    tok = lambda width: pl.BlockSpec((1, tb, width), lambda b, i: (b, i, 0))
    return pl.pallas_call(
        _router_kernel,
        grid=(bsz, s // tb),
        in_specs=[tok(d), vec, vec,
                  pl.BlockSpec((1, d), lambda b, i: (0, 0)),
                  pl.BlockSpec(w_q.shape, lambda b, i: (0, 0)),
                  pl.BlockSpec(keys.shape, lambda b, i: (0, 0, 0))],
        out_specs=[tok(d), tok(N_SEL), tok(N_SEL)],
        out_shape=[jax.ShapeDtypeStruct((bsz, s, d), BF16),
                   jax.ShapeDtypeStruct((bsz, s, N_SEL), jnp.int32),
                   jax.ShapeDtypeStruct((bsz, s, N_SEL), F32)],
        compiler_params=_cparams("arbitrary", "arbitrary"),
        name="peer_router",
    )(x, shift, scale, gain, w_q, keys)


def pack_table(tab):
    bits = lax.bitcast_convert_type(tab.astype(BF16), jnp.uint16).astype(jnp.uint32)
    words = bits[:, :HALF_D] | (bits[:, HALF_D:] << 16)
    return lax.bitcast_convert_type(words, jnp.int32).reshape(-1, LANES)


def _unpack(words):
    lo = pltpu.bitcast(words << 16, F32)
    hi = pltpu.bitcast(words & jnp.int32(-65536), F32)
    return lo, hi


def _gather_tile(idx_ref, base, tab_ref, g_ref, slot):
    for k in range(N_SEL):
        e4 = pl.multiple_of(idx_ref[base + k], ROWS_PER_EXPERT)
        g_ref[slot, pl.ds(k, ROWS_PER_EXPERT, stride=GATHER_STRIDE), :] = tab_ref[pl.ds(e4, ROWS_PER_EXPERT), :]


def _tile_halves(g_ref, slot, c):
    lo, hi = _unpack(g_ref[slot, pl.ds(c * GATHER_STRIDE, N_SEL), :])
    return lo.astype(BF16), hi.astype(BF16)


def _peer_u_kernel(idx_ref, h_ref, gate_ref, tab_ref, w_ref, g_ref, *, tb):
    def group(g, sub):
        r0 = pl.multiple_of((g * SUBGROUPS + sub) * SUBLANES, SUBLANES)
        x8 = h_ref[pl.ds(r0, SUBLANES), :]
        row = lax.broadcasted_iota(jnp.int32, (SUBLANES, N_SEL), 0)
        acc = jnp.zeros((SUBLANES, N_SEL), F32)
        for j in range(SUBLANES):
            slot = sub * SUBLANES + j
            _gather_tile(idx_ref, (r0 + j) * N_SEL, tab_ref, g_ref, slot)
            res = jnp.zeros((SUBLANES, N_SEL), F32)
            for c in range(ROWS_PER_EXPERT):
                lo, hi = _tile_halves(g_ref, slot, c)
                rows = jnp.concatenate([lo, hi], axis=1)
                xs = jnp.concatenate([x8[:, c * LANES:(c + 1) * LANES],
                                      x8[:, HALF_D + c * LANES:HALF_D + (c + 1) * LANES]], axis=1)
                res += lax.dot_general(xs, rows, NT_DIMS, preferred_element_type=F32)
            acc = jnp.where(row == j, res, acc)
        w_ref[pl.ds(r0, SUBLANES), :] = gate_ref[pl.ds(r0, SUBLANES), :] * jax.nn.gelu(acc)

    def body(g, carry):
        for sub in range(SUBGROUPS):
            group(g, sub)
        return carry

    lax.fori_loop(0, tb // (SUBLANES * SUBGROUPS), body, 0)


def peer_u_phase(idx_flat, h, gates, utab, *, tb=64):
    t = h.shape[0]
    return pl.pallas_call(
        functools.partial(_peer_u_kernel, tb=tb),
        grid=(t // tb,),
        in_specs=[pl.BlockSpec((tb * N_SEL,), lambda i: (i,), memory_space=pltpu.SMEM),
                  pl.BlockSpec((tb, D_MODEL), lambda i: (i, 0)),
                  pl.BlockSpec((tb, N_SEL), lambda i: (i, 0)),
                  pl.BlockSpec(memory_space=pltpu.VMEM)],
        out_specs=pl.BlockSpec((tb, N_SEL), lambda i: (i, 0)),
        out_shape=jax.ShapeDtypeStruct((t, N_SEL), F32),
        scratch_shapes=[pltpu.VMEM((SUBLANES * SUBGROUPS, ROWS_PER_EXPERT * GATHER_STRIDE, LANES), jnp.int32)],
        compiler_params=_cparams("arbitrary"),
        name="peer_u",
    )(idx_flat, h, gates, utab)


def _peer_v_kernel(idx_ref, w_ref, x_ref, gate1_ref, tab_ref, o_ref, g_ref, *, tb):
    def group(g, sub):
        r0 = pl.multiple_of((g * SUBGROUPS + sub) * SUBLANES, SUBLANES)
        w8 = w_ref[pl.ds(r0, SUBLANES), :]
        w_hi = w8.astype(BF16)
        w_lo = (w8 - w_hi.astype(F32)).astype(BF16)
        row = lax.broadcasted_iota(jnp.int32, (SUBLANES, N_SEL), 0)
        zero = jnp.zeros_like(w_hi)
        y = jnp.zeros((2 * SUBLANES, D_MODEL), F32)
        for j in range(SUBLANES):
            slot = sub * SUBLANES + j
            _gather_tile(idx_ref, (r0 + j) * N_SEL, tab_ref, g_ref, slot)
            wm = jnp.concatenate([jnp.where(row == j, w_hi, zero), jnp.where(row == j, w_lo, zero)], axis=0)
            halves = [_tile_halves(g_ref, slot, c) for c in range(ROWS_PER_EXPERT)]
            vals = jnp.concatenate([h[0] for h in halves] + [h[1] for h in halves], axis=1)
            y = y + jnp.dot(wm, vals, preferred_element_type=F32)
        o_ref[pl.ds(r0, SUBLANES), :] = x_ref[pl.ds(r0, SUBLANES), :] + gate1_ref[0] * (y[:SUBLANES] + y[SUBLANES:])

    def body(g, carry):
        for sub in range(SUBGROUPS):
            group(g, sub)
        return carry

    lax.fori_loop(0, tb // (SUBLANES * SUBGROUPS), body, 0)


def peer_v_phase(idx_flat, w, x, gate1, vtab, *, tb=64):
    t = x.shape[0]
    blocks_per_batch = (t // gate1.shape[0]) // tb
    return pl.pallas_call(
        functools.partial(_peer_v_kernel, tb=tb),
        grid=(t // tb,),
        in_specs=[pl.BlockSpec((tb * N_SEL,), lambda i: (i,), memory_space=pltpu.SMEM),
                  pl.BlockSpec((tb, N_SEL), lambda i: (i, 0)),
                  pl.BlockSpec((tb, D_MODEL), lambda i: (i, 0)),
                  pl.BlockSpec((1, 1, D_MODEL), lambda i: (i // blocks_per_batch, 0, 0)),
                  pl.BlockSpec(memory_space=pltpu.VMEM)],
        out_specs=pl.BlockSpec((tb, D_MODEL), lambda i: (i, 0)),
        out_shape=jax.ShapeDtypeStruct((t, D_MODEL), F32),
        scratch_shapes=[pltpu.VMEM((SUBLANES * SUBGROUPS, ROWS_PER_EXPERT * GATHER_STRIDE, LANES), jnp.int32)],
        compiler_params=_cparams("arbitrary"),
        name="peer_v",
    )(idx_flat, w, x, gate1, vtab)


def peer_layer(x, shift, scale, gate1, gain, w_q, keys, utab, vtab):
    bsz, s, d = x.shape
    t = bsz * s
    h, idx, gates = peer_router(x, shift, scale, gain, w_q, keys)
    idx_flat = idx.reshape(t * N_SEL)
    w = peer_u_phase(idx_flat, h.reshape(t, d), gates.reshape(t, N_SEL), utab)
    out = peer_v_phase(idx_flat, w, x.reshape(t, d), gate1, vtab)
    return out.reshape(bsz, s, d)


def _split3(x):
    a = x.astype(BF16)
    r = x - a.astype(F32)
    b = r.astype(BF16)
    c = (r - b.astype(F32)).astype(BF16)
    return a, b, c


def _kvq_kernel(x_ref, kshift_ref, kscale_ref, kgain_ref, qshift_ref, qscale_ref, qgain_ref,
                wkv_ref, wf_hi_ref, wf_lo_ref, fb_ref, wq_ref,
                q_ref, k_ref, v_ref, cf_ref, carry_ref, *, ts):
    d = D_MODEL
    x = x_ref[0]
    r = lax.rsqrt(jnp.mean(x * x, axis=-1, keepdims=True) + RMS_EPS)
    xn = x * r
    hk = xn * kgain_ref[...] * (1.0 + kscale_ref[0]) + kshift_ref[0]
    hq = xn * qgain_ref[...] * (1.0 + qscale_ref[0]) + qshift_ref[0]
    hk_hi = hk.astype(BF16)
    kv = jnp.dot(hk_hi, wkv_ref[...], preferred_element_type=F32)
    k_ref[0] = kv[:, :d].astype(BF16)
    v_ref[0] = kv[:, d:].astype(BF16)
    q = jnp.dot(hq.astype(BF16), wq_ref[...], preferred_element_type=F32)
    q_ref[0] = (q * (LOG2_E * HEAD_DIM ** -0.5)).astype(BF16)

    hk_lo = (hk - hk_hi.astype(F32)).astype(BF16)
    f_logit = (jnp.dot(hk_hi, wf_hi_ref[...], preferred_element_type=F32)
               + jnp.dot(hk_lo, wf_hi_ref[...], preferred_element_type=F32)
               + jnp.dot(hk_hi, wf_lo_ref[...], preferred_element_type=F32))
    z = f_logit + fb_ref[...]
    log_f = LOG2_E * (jnp.minimum(z, 0.0) - jnp.log1p(jnp.exp(-jnp.abs(z))))

    @pl.when(pl.program_id(1) == 0)
    def _():
        carry_ref[...] = jnp.zeros_like(carry_ref)

    tri = (lax.broadcasted_iota(jnp.int32, (ts, ts), 0)
           >= lax.broadcasted_iota(jnp.int32, (ts, ts), 1)).astype(BF16)
    cum = carry_ref[...]
    for piece in _split3(log_f):
        cum = cum + jnp.dot(tri, piece, preferred_element_type=F32)
    cf_ref[0] = cum
    carry_ref[...] = cum[ts - 1:ts, :]


def kvq_proj(x, kshift, kscale, kgain, qshift, qscale, qgain, w_kv, wf_hi, wf_lo, f_bias, w_q, *, ts=512):
    bsz, s, d = x.shape
    vec = pl.BlockSpec((1, 1, d), lambda b, i: (b, 0, 0))
    const = lambda a: pl.BlockSpec(a.shape, lambda b, i: (0,) * a.ndim)
    tok = lambda width: pl.BlockSpec((1, ts, width), lambda b, i: (b, i, 0))
    return pl.pallas_call(
        functools.partial(_kvq_kernel, ts=ts),
        grid=(bsz, s // ts),
        in_specs=[tok(d), vec, vec, const(kgain), vec, vec, const(qgain),
                  const(w_kv), const(wf_hi), const(wf_lo), const(f_bias), const(w_q)],
        out_specs=[tok(d), tok(d), tok(d), tok(LANES)],
        out_shape=[jax.ShapeDtypeStruct((bsz, s, d), BF16)] * 3
        + [jax.ShapeDtypeStruct((bsz, s, LANES), F32)],
        scratch_shapes=[pltpu.VMEM((1, LANES), F32)],
        compiler_params=_cparams("arbitrary", "arbitrary"),
        name="kvq_proj",
    )(x, kshift, kscale, kgain, qshift, qscale, qgain, w_kv, wf_hi, wf_lo, f_bias, w_q)


def _fox_kernel(q_ref, k_ref, v_ref, cfq_ref, cfk_ref, o_ref, q2_ref, cq_ref, m_ref, l_ref, acc_ref, *, tq):
    pair = pl.program_id(1)
    qi = pl.program_id(2)
    reps = tq // LANES
    lane = lax.broadcasted_iota(jnp.int32, (tq, LANES), 1)
    first_head = lane < HEAD_DIM
    q = q_ref[0]
    zq = jnp.zeros_like(q)
    q2_ref[0:tq, :] = jnp.where(first_head, q, zq)
    q2_ref[tq:2 * tq, :] = jnp.where(first_head, zq, q)
    cf = cfq_ref[0]
    for sub in range(2):
        col = jnp.sum(jnp.where(lane == 2 * pair + sub, cf, 0.0), axis=1, keepdims=True)
        cq_ref[sub * tq:(sub + 1) * tq, :] = jnp.broadcast_to(col, (tq, LANES))
    m_ref[...] = jnp.full_like(m_ref, -jnp.inf)
    l_ref[...] = jnp.zeros_like(l_ref)
    acc_ref[...] = jnp.zeros_like(acc_ref)
    below_diag = (lax.broadcasted_iota(jnp.int32, (tq, tq), 1)
                  <= lax.broadcasted_iota(jnp.int32, (tq, tq), 0))

    def block(j, masked):
        k0 = pl.multiple_of(j * tq, tq)
        kb = k_ref[0, pl.ds(k0, tq), :]
        vb = v_ref[0, pl.ds(k0, tq), :]
        s = lax.dot_general(q2_ref[...], kb, NT_DIMS, preferred_element_type=F32)
        ck = cfk_ref[0, 0, :, pl.ds(k0, tq)]
        ps, alphas = [], []
        for sub in range(2):
            rows = slice(sub * tq, (sub + 1) * tq)
            ss = s[rows] + pltpu.repeat(cq_ref[rows, :], reps, axis=1) - ck[sub:sub + 1, :]
            if masked:
                ss = jnp.where(below_diag, ss, -jnp.inf)
            m_prev = m_ref[rows, :]
            m_new = jnp.maximum(m_prev, jnp.max(ss, axis=1, keepdims=True))
            alpha = jnp.exp2(m_prev - m_new)
            p = jnp.exp2(ss - pltpu.repeat(m_new, reps, axis=1))
            l_ref[rows, :] = alpha * l_ref[rows, :] + jnp.sum(p, axis=1, keepdims=True)
            m_ref[rows, :] = m_new
            ps.append(p.astype(BF16))
            alphas.append(alpha)
        pv = jnp.dot(jnp.concatenate(ps, axis=0), vb, preferred_element_type=F32)
        acc_ref[...] = jnp.concatenate(alphas, axis=0) * acc_ref[...] + pv

    def full_block(j, carry):
        block(j, False)
        return carry

    lax.fori_loop(0, qi, full_block, 0)
    block(qi, True)
    o = jnp.where(first_head, acc_ref[0:tq, :] / l_ref[0:tq, :], acc_ref[tq:2 * tq, :] / l_ref[tq:2 * tq, :])
    o_ref[0] = o.astype(o_ref.dtype)


def fox_attention(q, k, v, cf_tok, cf_head, *, tq=512):
    bsz, s, d = q.shape
    n_pairs = N_HEADS // 2
    return pl.pallas_call(
        functools.partial(_fox_kernel, tq=tq),
        grid=(bsz, n_pairs, s // tq),
        in_specs=[pl.BlockSpec((1, tq, LANES), lambda b, p, i: (b, i, p)),
                  pl.BlockSpec((1, s, LANES), lambda b, p, i: (b, 0, p)),
                  pl.BlockSpec((1, s, LANES), lambda b, p, i: (b, 0, p)),
                  pl.BlockSpec((1, tq, LANES), lambda b, p, i: (b, i, 0)),
                  pl.BlockSpec((1, 1, 2, s), lambda b, p, i: (b, p, 0, 0))],
        out_specs=pl.BlockSpec((1, tq, LANES), lambda b, p, i: (b, i, p)),
        out_shape=jax.ShapeDtypeStruct((bsz, s, d), BF16),
        scratch_shapes=[pltpu.VMEM((2 * tq, LANES), BF16),
                        pltpu.VMEM((2 * tq, LANES), F32),
                        pltpu.VMEM((2 * tq, LANES), F32),
                        pltpu.VMEM((2 * tq, LANES), F32),
                        pltpu.VMEM((2 * tq, LANES), F32)],
        compiler_params=_cparams("arbitrary", "arbitrary", "arbitrary"),
        name="fox_attention",
    )(q, k, v, cf_tok, cf_head)


def _out_proj_kernel(a_ref, x_ref, gate1_ref, w_ref, o_ref):
    y = jnp.dot(a_ref[0], w_ref[...], preferred_element_type=F32)
    o_ref[0] = x_ref[0] + gate1_ref[0] * y


def out_proj(a, x, gate1, w, *, ts=512):
    bsz, s, d = x.shape
    tok = pl.BlockSpec((1, ts, d), lambda b, i: (b, i, 0))
    return pl.pallas_call(
        _out_proj_kernel,
        grid=(bsz, s // ts),
        in_specs=[tok, tok, pl.BlockSpec((1, 1, d), lambda b, i: (b, 0, 0)),
                  pl.BlockSpec((d, d), lambda b, i: (0, 0))],
        out_specs=tok,
        out_shape=jax.ShapeDtypeStruct((bsz, s, d), F32),
        compiler_params=_cparams("arbitrary", "arbitrary"),
        name="out_proj",
    )(a, x, gate1, w)


def _final_norm_kernel(x_ref, gain_ref, o_ref):
    x = x_ref[...]
    r = lax.rsqrt(jnp.mean(x * x, axis=-1, keepdims=True) + RMS_EPS)
    o_ref[...] = (x * r) * gain_ref[...]


def final_norm_call(x, gain, *, ts=1024):
    t, d = x.shape
    return pl.pallas_call(
        _final_norm_kernel,
        grid=(t // ts,),
        in_specs=[pl.BlockSpec((ts, d), lambda i: (i, 0)), pl.BlockSpec((1, d), lambda i: (0, 0))],
        out_specs=pl.BlockSpec((ts, d), lambda i: (i, 0)),
        out_shape=jax.ShapeDtypeStruct((t, d), F32),
        compiler_params=_cparams("arbitrary"),
        name="final_norm",
    )(x, gain)


def kernel(x, c, mix_norm, mix_ada_w, mix_ada_b, ffn_norm, ffn_ada_w, ffn_ada_b, a_w_in, a_conv, a_w_out, kv_norm, kv_ada_w, kv_ada_b, kv_w, f_bias, b_w_q, b_w_o, peer_w_q, peer_sub_keys, peer_u, peer_v, final_norm):
    bsz, s, d = x.shape
    depth = mix_norm.shape[0]
    n_a = a_w_in.shape[0]

    ada_w = jnp.concatenate([mix_ada_w[l] for l in range(depth)] + [ffn_ada_w[l] for l in range(depth)]
                            + [kv_ada_w], axis=1)
    ada_b = jnp.concatenate([mix_ada_b[l] for l in range(depth)] + [ffn_ada_b[l] for l in range(depth)]
                            + [kv_ada_b], axis=0)
    ada = ada_linear(c, ada_w, ada_b)

    def ada_chunk(i):
        return ada[:, i * d:(i + 1) * d].reshape(bsz, 1, d)

    def mix_ada(l):
        return ada_chunk(3 * l), ada_chunk(3 * l + 1), 1.0 + ada_chunk(3 * l + 2)

    def ffn_ada(l):
        o = 3 * depth + 3 * l
        return ada_chunk(o), ada_chunk(o + 1), 1.0 + ada_chunk(o + 2)

    kv_shift, kv_scale = ada_chunk(6 * depth), ada_chunk(6 * depth + 1)

    kv = None
    for layer in range(depth):
        shift, scale, gate1 = mix_ada(layer)
        gain = mix_norm[layer].reshape(1, d)
        if layer < n_a:
            x = mixer_a(x, shift, scale, gate1, gain, a_w_in[layer].astype(BF16), a_conv[layer],
                        a_w_out[layer].astype(BF16))
        else:
            j = layer - n_a
            if j == 0:
                w_f = jnp.pad(kv_w[:, 2 * d:], ((0, 0), (0, LANES - N_HEADS)))
                wf_hi = w_f.astype(BF16)
                wf_lo = (w_f - wf_hi.astype(F32)).astype(BF16)
                fb = jnp.pad(f_bias, (0, LANES - N_HEADS)).reshape(1, LANES)
                q, k, v, cf = kvq_proj(x, kv_shift, kv_scale, kv_norm.reshape(1, d), shift, scale, gain,
                                       kv_w[:, :2 * d].astype(BF16), wf_hi, wf_lo, fb, b_w_q[j].astype(BF16))
                kv = (k, v, cf, jnp.swapaxes(cf[:, :, :N_HEADS], 1, 2).reshape(bsz, N_HEADS // 2, 2, s))
            else:
                raise NotImplementedError("more than one attention layer")
            attn = fox_attention(q, kv[0], kv[1], kv[2], kv[3])
            x = out_proj(attn, x, gate1, b_w_o[j].astype(BF16))
        shift, scale, gate1 = ffn_ada(layer)
        keys = peer_sub_keys[layer].reshape(2 * PEER_HEADS, PEER_N_KEYS, -1).astype(BF16)
        x = peer_layer(x, shift, scale, gate1, ffn_norm[layer].reshape(1, d), peer_w_q[layer].astype(BF16),
                       keys, pack_table(peer_u[layer]), pack_table(peer_v[layer]))
    return final_norm_call(x.reshape(bsz * s, d), final_norm.reshape(1, d)).reshape(bsz, s, d)
```

```python
import functools

import jax
import jax.numpy as jnp
from jax import lax
from jax.experimental import pallas as pl
from jax.experimental.pallas import tpu as pltpu

D_MODEL = 1024
N_HEADS = 16
HEAD_DIM = 64
PEER_HEADS = 8
PEER_N_KEYS = 128
PEER_TOPK = 16
N_SEL = PEER_HEADS * PEER_TOPK
RMS_EPS = 1e-6
LOG2_E = 1.4426950408889634

LANES = 128
SUBLANES = 8
HALF_D = D_MODEL // 2
ROWS_PER_EXPERT = HALF_D // LANES
IDX_SPLIT = 8
SUBGROUPS = 8
GATHER_STRIDE = 136
VMEM_LIMIT = 56 * 1024 * 1024

BF16 = jnp.bfloat16
F32 = jnp.float32
NT_DIMS = (((1,), (1,)), ((), ()))


def _cparams(*sem):
    return pltpu.CompilerParams(dimension_semantics=sem, vmem_limit_bytes=VMEM_LIMIT)


def _norm_mod(x, gain, shift, scale):
    r = lax.rsqrt(jnp.mean(x * x, axis=-1, keepdims=True) + RMS_EPS)
    return (x * r) * gain * (1.0 + scale) + shift


def _ada_kernel(c_ref, w_ref, b_ref, o_ref):
    c = c_ref[...]
    sc = c * jax.nn.sigmoid(c)
    o_ref[...] = jnp.dot(sc, w_ref[...], precision=lax.Precision.HIGHEST,
                         preferred_element_type=F32) + b_ref[...]


def ada_linear(c, w, b, *, tn=1024):
    bsz, d = c.shape
    n = w.shape[1]
    return pl.pallas_call(
        _ada_kernel,
        grid=(n // tn,),
        in_specs=[pl.BlockSpec((bsz, d), lambda j: (0, 0)),
                  pl.BlockSpec((d, tn), lambda j: (0, j)),
                  pl.BlockSpec((1, tn), lambda j: (0, j))],
        out_specs=pl.BlockSpec((bsz, tn), lambda j: (0, j)),
        out_shape=jax.ShapeDtypeStruct((bsz, n), F32),
        compiler_params=_cparams("arbitrary"),
        name="ada_linear",
    )(c, w, b.reshape(1, n))


def _mixer_a_kernel(x_ref, shift_ref, scale_ref, gate1_ref, gain_ref, win_ref, conv_ref, wout_ref,
                    o_ref, cu_ref, *, ts):
    d = D_MODEL
    x = x_ref[0]
    h = _norm_mod(x, gain_ref[...], shift_ref[0], scale_ref[0])
    proj = jnp.dot(h.astype(BF16), win_ref[...], preferred_element_type=F32)
    b_gate = proj[:, :d]
    cu = proj[:, d:2 * d] * proj[:, 2 * d:]

    @pl.when(pl.program_id(1) == 0)
    def _():
        cu_ref[0:SUBLANES, :] = jnp.zeros((SUBLANES, d), F32)

    cu_ref[SUBLANES:SUBLANES + ts, :] = cu
    w = conv_ref[...]
    z = (w[0:1] * cu_ref[SUBLANES - 2:SUBLANES - 2 + ts, :]
         + w[1:2] * cu_ref[SUBLANES - 1:SUBLANES - 1 + ts, :]
         + w[2:3] * cu)
    cu_ref[0:SUBLANES, :] = cu_ref[ts:ts + SUBLANES, :]
    y = jnp.dot((b_gate * z).astype(BF16), wout_ref[...], preferred_element_type=F32)
    o_ref[0] = x + gate1_ref[0] * y


def mixer_a(x, shift, scale, gate1, gain, w_in, conv_w, w_out, *, ts=256):
    bsz, s, d = x.shape
    vec = pl.BlockSpec((1, 1, d), lambda b, i: (b, 0, 0))
    return pl.pallas_call(
        functools.partial(_mixer_a_kernel, ts=ts),
        grid=(bsz, s // ts),
        in_specs=[pl.BlockSpec((1, ts, d), lambda b, i: (b, i, 0)), vec, vec, vec,
                  pl.BlockSpec((1, d), lambda b, i: (0, 0)),
                  pl.BlockSpec((d, 3 * d), lambda b, i: (0, 0)),
                  pl.BlockSpec((3, d), lambda b, i: (0, 0)),
                  pl.BlockSpec((d, d), lambda b, i: (0, 0))],
        out_specs=pl.BlockSpec((1, ts, d), lambda b, i: (b, i, 0)),
        out_shape=jax.ShapeDtypeStruct((bsz, s, d), F32),
        scratch_shapes=[pltpu.VMEM((ts + 2 * SUBLANES, d), F32)],
        compiler_params=_cparams("arbitrary", "arbitrary"),
        name="mixer_a",
    )(x, shift, scale, gate1, gain, w_in, conv_w, w_out)


def _topk_rows(s, k, payload=None):
    n = s.shape[0]
    rows = lax.broadcasted_iota(jnp.int32, s.shape, 0).astype(F32)
    vals, idxs, pays = [], [], []
    for _ in range(k):
        m = jnp.max(s, axis=0, keepdims=True)
        am = jnp.min(jnp.where(s == m, rows, float(n)), axis=0, keepdims=True)
        hit = rows == am
        vals.append(m)
        idxs.append(am)
        if payload is not None:
            pays.append(jnp.sum(jnp.where(hit, payload, 0), axis=0, keepdims=True))
        s = jnp.where(hit, -jnp.inf, s)
    return vals, idxs, pays


def _router_kernel(x_ref, shift_ref, scale_ref, gain_ref, wq_ref, keys_ref, h_ref, idx_ref, gates_ref):
    x = x_ref[0]
    h = _norm_mod(x, gain_ref[...], shift_ref[0], scale_ref[0])
    hb = h.astype(BF16)
    h_ref[0] = hb
    q = jnp.dot(hb, wq_ref[...], preferred_element_type=F32).astype(BF16)
    idx_rows, gate_rows = [], []
    for head in range(PEER_HEADS):
        top_s, top_i = [], []
        for part in range(2):
            hp = head * 2 + part
            st = lax.dot_general(keys_ref[hp], q[:, hp * LANES:(hp + 1) * LANES], NT_DIMS,
                                 preferred_element_type=F32)
            vals, idxs, _ = _topk_rows(st, PEER_TOPK)
            top_s.append(vals)
            top_i.append(idxs)
        s2 = jnp.concatenate(top_s[1], axis=0)
        i2 = jnp.concatenate(top_i[1], axis=0).astype(jnp.int32)
        n_b = [PEER_TOPK // (a + 1) for a in range(PEER_TOPK)]
        pad = -(-sum(n_b) // SUBLANES) * SUBLANES - sum(n_b)
        cand_s = jnp.concatenate([top_s[0][a] + s2[:n_b[a]] for a in range(PEER_TOPK)]
                                 + [jnp.full((pad, s2.shape[1]), -jnp.inf, F32)], axis=0)
        cand_i = jnp.concatenate([top_i[0][a].astype(jnp.int32) * PEER_N_KEYS + i2[:n_b[a]]
                                  for a in range(PEER_TOPK)]
                                 + [jnp.zeros((pad, s2.shape[1]), jnp.int32)], axis=0)
        best_s, _, best_e = _topk_rows(cand_s, PEER_TOPK, payload=cand_i)
        bs = jnp.concatenate(best_s, axis=0)
        e = jnp.exp(bs - bs[0:1])
        gate_rows.append(e / jnp.sum(e, axis=0, keepdims=True))
        idx_rows.append(jnp.concatenate(best_e, axis=0))
    idx_ref[0] = (jnp.concatenate(idx_rows, axis=0) * ROWS_PER_EXPERT).T
    gates_ref[0] = jnp.concatenate(gate_rows, axis=0).T


def peer_router(x, shift, scale, gain, w_q, keys, *, tb=256):
    bsz, s, d = x.shape
    vec = pl.BlockSpec((1, 1, d), lambda b, i: (b, 0, 0))
    tok = lambda width: pl.BlockSpec((1, tb, width), lambda b, i: (b, i, 0))
    return pl.pallas_call(
        _router_kernel,
        grid=(bsz, s // tb),
        in_specs=[tok(d), vec, vec,
                  pl.BlockSpec((1, d), lambda b, i: (0, 0)),
                  pl.BlockSpec(w_q.shape, lambda b, i: (0, 0)),
                  pl.BlockSpec(keys.shape, lambda b, i: (0, 0, 0))],
        out_specs=[tok(d), tok(N_SEL), tok(N_SEL)],
        out_shape=[jax.ShapeDtypeStruct((bsz, s, d), BF16),
                   jax.ShapeDtypeStruct((bsz, s, N_SEL), jnp.int32),
                   jax.ShapeDtypeStruct((bsz, s, N_SEL), F32)],
        compiler_params=_cparams("arbitrary", "arbitrary"),
        name="peer_router",
    )(x, shift, scale, gain, w_q, keys)


def pack_table(tab):
    bits = lax.bitcast_convert_type(tab.astype(BF16), jnp.uint16).astype(jnp.uint32)
    words = bits[:, :HALF_D] | (bits[:, HALF_D:] << 16)
    return lax.bitcast_convert_type(words, jnp.int32).reshape(-1, LANES)


def _unpack(words):
    lo = pltpu.bitcast(words << 16, F32)
    hi = pltpu.bitcast(words & jnp.int32(-65536), F32)
    return lo, hi


def _gather_tile(idx_refs, tok, tab_ref, g_ref, slot):
    per = N_SEL // IDX_SPLIT
    base = tok * per
    for m in range(per):
        for a in range(IDX_SPLIT):
            k = a * per + m
            e4 = pl.multiple_of(idx_refs[a][base + m], ROWS_PER_EXPERT)
            g_ref[slot, pl.ds(k, ROWS_PER_EXPERT, stride=GATHER_STRIDE), :] = tab_ref[pl.ds(e4, ROWS_PER_EXPERT), :]


def _tile_halves(g_ref, slot, c):
    lo, hi = _unpack(g_ref[slot, pl.ds(c * GATHER_STRIDE, N_SEL), :])
    return lo.astype(BF16), hi.astype(BF16)


def _peer_u_kernel(*refs, tb):
    idx_refs = refs[:IDX_SPLIT]
    h_ref, gate_ref, tab_ref, w_ref, g_ref = refs[IDX_SPLIT:]

    def group(g, sub):
        r0 = pl.multiple_of((g * SUBGROUPS + sub) * SUBLANES, SUBLANES)
        x8 = h_ref[pl.ds(r0, SUBLANES), :]
        row = lax.broadcasted_iota(jnp.int32, (SUBLANES, N_SEL), 0)
        acc = jnp.zeros((SUBLANES, N_SEL), F32)
        for j in range(SUBLANES):
            slot = sub * SUBLANES + j
            _gather_tile(idx_refs, r0 + j, tab_ref, g_ref, slot)
            res = jnp.zeros((SUBLANES, N_SEL), F32)
            for c in range(ROWS_PER_EXPERT):
                lo, hi = _tile_halves(g_ref, slot, c)
                rows = jnp.concatenate([lo, hi], axis=1)
                xs = jnp.concatenate([x8[:, c * LANES:(c + 1) * LANES],
                                      x8[:, HALF_D + c * LANES:HALF_D + (c + 1) * LANES]], axis=1)
                res += lax.dot_general(xs, rows, NT_DIMS, preferred_element_type=F32)
            acc = jnp.where(row == j, res, acc)
        w_ref[pl.ds(r0, SUBLANES), :] = gate_ref[pl.ds(r0, SUBLANES), :] * jax.nn.gelu(acc)

    def body(g, carry):
        for sub in range(SUBGROUPS):
            group(g, sub)
        return carry

    lax.fori_loop(0, tb // (SUBLANES * SUBGROUPS), body, 0)


def _idx_specs(tb):
    return [pl.BlockSpec((tb * (N_SEL // IDX_SPLIT),), lambda i: (i,), memory_space=pltpu.SMEM)] * IDX_SPLIT


def peer_u_phase(idx_parts, h, gates, utab, *, tb=64):
    t = h.shape[0]
    return pl.pallas_call(
        functools.partial(_peer_u_kernel, tb=tb),
        grid=(t // tb,),
        in_specs=_idx_specs(tb) + [
                  pl.BlockSpec((tb, D_MODEL), lambda i: (i, 0)),
                  pl.BlockSpec((tb, N_SEL), lambda i: (i, 0)),
                  pl.BlockSpec(memory_space=pltpu.VMEM)],
        out_specs=pl.BlockSpec((tb, N_SEL), lambda i: (i, 0)),
        out_shape=jax.ShapeDtypeStruct((t, N_SEL), F32),
        scratch_shapes=[pltpu.VMEM((SUBLANES * SUBGROUPS, ROWS_PER_EXPERT * GATHER_STRIDE, LANES), jnp.int32)],
        compiler_params=_cparams("arbitrary"),
        name="peer_u",
    )(*idx_parts, h, gates, utab)


def _peer_v_kernel(*refs, tb):
    idx_refs = refs[:IDX_SPLIT]
    w_ref, x_ref, gate1_ref, tab_ref, o_ref, g_ref = refs[IDX_SPLIT:]

    def group(g, sub):
        r0 = pl.multiple_of((g * SUBGROUPS + sub) * SUBLANES, SUBLANES)
        w8 = w_ref[pl.ds(r0, SUBLANES), :]
        w_hi = w8.astype(BF16)
        w_lo = (w8 - w_hi.astype(F32)).astype(BF16)
        row = lax.broadcasted_iota(jnp.int32, (SUBLANES, N_SEL), 0)
        zero = jnp.zeros_like(w_hi)
        y = jnp.zeros((2 * SUBLANES, D_MODEL), F32)
        for j in range(SUBLANES):
            slot = sub * SUBLANES + j
            _gather_tile(idx_refs, r0 + j, tab_ref, g_ref, slot)
            wm = jnp.concatenate([jnp.where(row == j, w_hi, zero), jnp.where(row == j, w_lo, zero)], axis=0)
            halves = [_tile_halves(g_ref, slot, c) for c in range(ROWS_PER_EXPERT)]
            vals = jnp.concatenate([h[0] for h in halves] + [h[1] for h in halves], axis=1)
            y = y + jnp.dot(wm, vals, preferred_element_type=F32)
        o_ref[pl.ds(r0, SUBLANES), :] = x_ref[pl.ds(r0, SUBLANES), :] + gate1_ref[0] * (y[:SUBLANES] + y[SUBLANES:])

    def body(g, carry):
        for sub in range(SUBGROUPS):
            group(g, sub)
        return carry

    lax.fori_loop(0, tb // (SUBLANES * SUBGROUPS), body, 0)


def peer_v_phase(idx_parts, w, x, gate1, vtab, *, tb=64):
    t = x.shape[0]
    blocks_per_batch = (t // gate1.shape[0]) // tb
    return pl.pallas_call(
        functools.partial(_peer_v_kernel, tb=tb),
        grid=(t // tb,),
        in_specs=_idx_specs(tb) + [
                  pl.BlockSpec((tb, N_SEL), lambda i: (i, 0)),
                  pl.BlockSpec((tb, D_MODEL), lambda i: (i, 0)),
                  pl.BlockSpec((1, 1, D_MODEL), lambda i: (i // blocks_per_batch, 0, 0)),
                  pl.BlockSpec(memory_space=pltpu.VMEM)],
        out_specs=pl.BlockSpec((tb, D_MODEL), lambda i: (i, 0)),
        out_shape=jax.ShapeDtypeStruct((t, D_MODEL), F32),
        scratch_shapes=[pltpu.VMEM((SUBLANES * SUBGROUPS, ROWS_PER_EXPERT * GATHER_STRIDE, LANES), jnp.int32)],
        compiler_params=_cparams("arbitrary"),
        name="peer_v",
    )(*idx_parts, w, x, gate1, vtab)


def peer_layer(x, shift, scale, gate1, gain, w_q, keys, utab, vtab):
    bsz, s, d = x.shape
    t = bsz * s
    h, idx, gates = peer_router(x, shift, scale, gain, w_q, keys)
    idx = idx.reshape(t, IDX_SPLIT, N_SEL // IDX_SPLIT)
    idx_parts = [idx[:, a, :].reshape(-1) for a in range(IDX_SPLIT)]
    w = peer_u_phase(idx_parts, h.reshape(t, d), gates.reshape(t, N_SEL), utab)
    out = peer_v_phase(idx_parts, w, x.reshape(t, d), gate1, vtab)
    return out.reshape(bsz, s, d)


def _split3(x):
    a = x.astype(BF16)
    r = x - a.astype(F32)
    b = r.astype(BF16)
    c = (r - b.astype(F32)).astype(BF16)
    return a, b, c


def _kvq_kernel(x_ref, kshift_ref, kscale_ref, kgain_ref, qshift_ref, qscale_ref, qgain_ref,
                wkv_ref, wf_hi_ref, wf_lo_ref, fb_ref, wq_ref,
                q_ref, k_ref, v_ref, cf_ref, carry_ref, *, ts):
    d = D_MODEL
    x = x_ref[0]
    r = lax.rsqrt(jnp.mean(x * x, axis=-1, keepdims=True) + RMS_EPS)
    xn = x * r
    hk = xn * kgain_ref[...] * (1.0 + kscale_ref[0]) + kshift_ref[0]
    hq = xn * qgain_ref[...] * (1.0 + qscale_ref[0]) + qshift_ref[0]
    hk_hi = hk.astype(BF16)
    kv = jnp.dot(hk_hi, wkv_ref[...], preferred_element_type=F32)
    k_ref[0] = kv[:, :d].astype(BF16)
    v_ref[0] = kv[:, d:].astype(BF16)
    q = jnp.dot(hq.astype(BF16), wq_ref[...], preferred_element_type=F32)
    q_ref[0] = (q * (LOG2_E * HEAD_DIM ** -0.5)).astype(BF16)

    hk_lo = (hk - hk_hi.astype(F32)).astype(BF16)
    f_logit = (jnp.dot(hk_hi, wf_hi_ref[...], preferred_element_type=F32)
               + jnp.dot(hk_lo, wf_hi_ref[...], preferred_element_type=F32)
               + jnp.dot(hk_hi, wf_lo_ref[...], preferred_element_type=F32))
    z = f_logit + fb_ref[...]
    log_f = LOG2_E * (jnp.minimum(z, 0.0) - jnp.log1p(jnp.exp(-jnp.abs(z))))

    @pl.when(pl.program_id(1) == 0)
    def _():
        carry_ref[...] = jnp.zeros_like(carry_ref)

    tri = (lax.broadcasted_iota(jnp.int32, (ts, ts), 0)
           >= lax.broadcasted_iota(jnp.int32, (ts, ts), 1)).astype(BF16)
    cum = carry_ref[...]
    for piece in _split3(log_f):
        cum = cum + jnp.dot(tri, piece, preferred_element_type=F32)
    cf_ref[0] = cum
    carry_ref[...] = cum[ts - 1:ts, :]


def kvq_proj(x, kshift, kscale, kgain, qshift, qscale, qgain, w_kv, wf_hi, wf_lo, f_bias, w_q, *, ts=512):
    bsz, s, d = x.shape
    vec = pl.BlockSpec((1, 1, d), lambda b, i: (b, 0, 0))
    const = lambda a: pl.BlockSpec(a.shape, lambda b, i: (0,) * a.ndim)
    tok = lambda width: pl.BlockSpec((1, ts, width), lambda b, i: (b, i, 0))
    return pl.pallas_call(
        functools.partial(_kvq_kernel, ts=ts),
        grid=(bsz, s // ts),
        in_specs=[tok(d), vec, vec, const(kgain), vec, vec, const(qgain),
                  const(w_kv), const(wf_hi), const(wf_lo), const(f_bias), const(w_q)],
        out_specs=[tok(d), tok(d), tok(d), tok(LANES)],
        out_shape=[jax.ShapeDtypeStruct((bsz, s, d), BF16)] * 3
        + [jax.ShapeDtypeStruct((bsz, s, LANES), F32)],
        scratch_shapes=[pltpu.VMEM((1, LANES), F32)],
        compiler_params=_cparams("arbitrary", "arbitrary"),
        name="kvq_proj",
    )(x, kshift, kscale, kgain, qshift, qscale, qgain, w_kv, wf_hi, wf_lo, f_bias, w_q)


def _fox_kernel(q_ref, k_ref, v_ref, cfq_ref, cfk_ref, o_ref, q2_ref, cq_ref, m_ref, l_ref, acc_ref, *, tq):
    pair = pl.program_id(1)
    qi = pl.program_id(2)
    reps = tq // LANES
    lane = lax.broadcasted_iota(jnp.int32, (tq, LANES), 1)
    first_head = lane < HEAD_DIM
    q = q_ref[0]
    zq = jnp.zeros_like(q)
    q2_ref[0:tq, :] = jnp.where(first_head, q, zq)
    q2_ref[tq:2 * tq, :] = jnp.where(first_head, zq, q)
    cf = cfq_ref[0]
    for sub in range(2):
        col = jnp.sum(jnp.where(lane == 2 * pair + sub, cf, 0.0), axis=1, keepdims=True)
        cq_ref[sub * tq:(sub + 1) * tq, :] = jnp.broadcast_to(col, (tq, LANES))
    m_ref[...] = jnp.full_like(m_ref, -jnp.inf)
    l_ref[...] = jnp.zeros_like(l_ref)
    acc_ref[...] = jnp.zeros_like(acc_ref)
    below_diag = (lax.broadcasted_iota(jnp.int32, (tq, tq), 1)
                  <= lax.broadcasted_iota(jnp.int32, (tq, tq), 0))

    def block(j, masked):
        k0 = pl.multiple_of(j * tq, tq)
        kb = k_ref[0, pl.ds(k0, tq), :]
        vb = v_ref[0, pl.ds(k0, tq), :]
        s = lax.dot_general(q2_ref[...], kb, NT_DIMS, preferred_element_type=F32)
        ck = cfk_ref[0, 0, :, pl.ds(k0, tq)]
        ps, alphas = [], []
        for sub in range(2):
            rows = slice(sub * tq, (sub + 1) * tq)
            ss = s[rows] + pltpu.repeat(cq_ref[rows, :], reps, axis=1) - ck[sub:sub + 1, :]
            if masked:
                ss = jnp.where(below_diag, ss, -jnp.inf)
            m_prev = m_ref[rows, :]
            m_new = jnp.maximum(m_prev, jnp.max(ss, axis=1, keepdims=True))
            alpha = jnp.exp2(m_prev - m_new)
            p = jnp.exp2(ss - pltpu.repeat(m_new, reps, axis=1))
            l_ref[rows, :] = alpha * l_ref[rows, :] + jnp.sum(p, axis=1, keepdims=True)
            m_ref[rows, :] = m_new
            ps.append(p.astype(BF16))
            alphas.append(alpha)
        pv = jnp.dot(jnp.concatenate(ps, axis=0), vb, preferred_element_type=F32)
        acc_ref[...] = jnp.concatenate(alphas, axis=0) * acc_ref[...] + pv

    def full_block(j, carry):
        block(j, False)
        return carry

    lax.fori_loop(0, qi, full_block, 0)
    block(qi, True)
    o = jnp.where(first_head, acc_ref[0:tq, :] / l_ref[0:tq, :], acc_ref[tq:2 * tq, :] / l_ref[tq:2 * tq, :])
    o_ref[0] = o.astype(o_ref.dtype)


def fox_attention(q, k, v, cf_tok, cf_head, *, tq=512):
    bsz, s, d = q.shape
    n_pairs = N_HEADS // 2
    return pl.pallas_call(
        functools.partial(_fox_kernel, tq=tq),
        grid=(bsz, n_pairs, s // tq),
        in_specs=[pl.BlockSpec((1, tq, LANES), lambda b, p, i: (b, i, p)),
                  pl.BlockSpec((1, s, LANES), lambda b, p, i: (b, 0, p)),
                  pl.BlockSpec((1, s, LANES), lambda b, p, i: (b, 0, p)),
                  pl.BlockSpec((1, tq, LANES), lambda b, p, i: (b, i, 0)),
                  pl.BlockSpec((1, 1, 2, s), lambda b, p, i: (b, p, 0, 0))],
        out_specs=pl.BlockSpec((1, tq, LANES), lambda b, p, i: (b, i, p)),
        out_shape=jax.ShapeDtypeStruct((bsz, s, d), BF16),
        scratch_shapes=[pltpu.VMEM((2 * tq, LANES), BF16),
                        pltpu.VMEM((2 * tq, LANES), F32),
                        pltpu.VMEM((2 * tq, LANES), F32),
                        pltpu.VMEM((2 * tq, LANES), F32),
                        pltpu.VMEM((2 * tq, LANES), F32)],
        compiler_params=_cparams("arbitrary", "arbitrary", "arbitrary"),
        name="fox_attention",
    )(q, k, v, cf_tok, cf_head)


def _out_proj_kernel(a_ref, x_ref, gate1_ref, w_ref, o_ref):
    y = jnp.dot(a_ref[0], w_ref[...], preferred_element_type=F32)
    o_ref[0] = x_ref[0] + gate1_ref[0] * y


def out_proj(a, x, gate1, w, *, ts=512):
    bsz, s, d = x.shape
    tok = pl.BlockSpec((1, ts, d), lambda b, i: (b, i, 0))
    return pl.pallas_call(
        _out_proj_kernel,
        grid=(bsz, s // ts),
        in_specs=[tok, tok, pl.BlockSpec((1, 1, d), lambda b, i: (b, 0, 0)),
                  pl.BlockSpec((d, d), lambda b, i: (0, 0))],
        out_specs=tok,
        out_shape=jax.ShapeDtypeStruct((bsz, s, d), F32),
        compiler_params=_cparams("arbitrary", "arbitrary"),
        name="out_proj",
    )(a, x, gate1, w)


def _final_norm_kernel(x_ref, gain_ref, o_ref):
    x = x_ref[...]
    r = lax.rsqrt(jnp.mean(x * x, axis=-1, keepdims=True) + RMS_EPS)
    o_ref[...] = (x * r) * gain_ref[...]


def final_norm_call(x, gain, *, ts=1024):
    t, d = x.shape
    return pl.pallas_call(
        _final_norm_kernel,
        grid=(t // ts,),
        in_specs=[pl.BlockSpec((ts, d), lambda i: (i, 0)), pl.BlockSpec((1, d), lambda i: (0, 0))],
        out_specs=pl.BlockSpec((ts, d), lambda i: (i, 0)),
        out_shape=jax.ShapeDtypeStruct((t, d), F32),
        compiler_params=_cparams("arbitrary"),
        name="final_norm",
    )(x, gain)


def kernel(x, c, mix_norm, mix_ada_w, mix_ada_b, ffn_norm, ffn_ada_w, ffn_ada_b, a_w_in, a_conv, a_w_out, kv_norm, kv_ada_w, kv_ada_b, kv_w, f_bias, b_w_q, b_w_o, peer_w_q, peer_sub_keys, peer_u, peer_v, final_norm):
    bsz, s, d = x.shape
    depth = mix_norm.shape[0]
    n_a = a_w_in.shape[0]

    ada_w = jnp.concatenate([mix_ada_w[l] for l in range(depth)] + [ffn_ada_w[l] for l in range(depth)]
                            + [kv_ada_w], axis=1)
    ada_b = jnp.concatenate([mix_ada_b[l] for l in range(depth)] + [ffn_ada_b[l] for l in range(depth)]
                            + [kv_ada_b], axis=0)
    ada = ada_linear(c, ada_w, ada_b)

    def ada_chunk(i):
        return ada[:, i * d:(i + 1) * d].reshape(bsz, 1, d)

    def mix_ada(l):
        return ada_chunk(3 * l), ada_chunk(3 * l + 1), 1.0 + ada_chunk(3 * l + 2)

    def ffn_ada(l):
        o = 3 * depth + 3 * l
        return ada_chunk(o), ada_chunk(o + 1), 1.0 + ada_chunk(o + 2)

    kv_shift, kv_scale = ada_chunk(6 * depth), ada_chunk(6 * depth + 1)

    kv = None
    for layer in range(depth):
        shift, scale, gate1 = mix_ada(layer)
        gain = mix_norm[layer].reshape(1, d)
        if layer < n_a:
            x = mixer_a(x, shift, scale, gate1, gain, a_w_in[layer].astype(BF16), a_conv[layer],
                        a_w_out[layer].astype(BF16))
        else:
            j = layer - n_a
            if j == 0:
                w_f = jnp.pad(kv_w[:, 2 * d:], ((0, 0), (0, LANES - N_HEADS)))
                wf_hi = w_f.astype(BF16)
                wf_lo = (w_f - wf_hi.astype(F32)).astype(BF16)
                fb = jnp.pad(f_bias, (0, LANES - N_HEADS)).reshape(1, LANES)
                q, k, v, cf = kvq_proj(x, kv_shift, kv_scale, kv_norm.reshape(1, d), shift, scale, gain,
                                       kv_w[:, :2 * d].astype(BF16), wf_hi, wf_lo, fb, b_w_q[j].astype(BF16))
                kv = (k, v, cf, jnp.swapaxes(cf[:, :, :N_HEADS], 1, 2).reshape(bsz, N_HEADS // 2, 2, s))
            else:
                raise NotImplementedError("more than one attention layer")
            attn = fox_attention(q, kv[0], kv[1], kv[2], kv[3])
            x = out_proj(attn, x, gate1, b_w_o[j].astype(BF16))
        shift, scale, gate1 = ffn_ada(layer)
        keys = peer_sub_keys[layer].reshape(2 * PEER_HEADS, PEER_N_KEYS, -1).astype(BF16)
        x = peer_layer(x, shift, scale, gate1, ffn_norm[layer].reshape(1, d), peer_w_q[layer].astype(BF16),
                       keys, pack_table(peer_u[layer]), pack_table(peer_v[layer]))
    return final_norm_call(x.reshape(bsz * s, d), final_norm.reshape(1, d)).reshape(bsz, s, d)
```

```python
import functools

import jax
import jax.numpy as jnp
from jax import lax
from jax.experimental import pallas as pl
from jax.experimental.pallas import tpu as pltpu

D_MODEL = 1024
N_HEADS = 16
HEAD_DIM = 64
PEER_HEADS = 8
PEER_N_KEYS = 128
PEER_TOPK = 16
N_SEL = PEER_HEADS * PEER_TOPK
RMS_EPS = 1e-6
LOG2_E = 1.4426950408889634

LANES = 128
SUBLANES = 8
HALF_D = D_MODEL // 2
ROWS_PER_EXPERT = HALF_D // LANES
IDX_SPLIT = 8
SUBGROUPS = 8
GATHER_STRIDE = 136
VMEM_LIMIT = 56 * 1024 * 1024

BF16 = jnp.bfloat16
F32 = jnp.float32
NT_DIMS = (((1,), (1,)), ((), ()))


def _cparams(*sem):
    return pltpu.CompilerParams(dimension_semantics=sem, vmem_limit_bytes=VMEM_LIMIT)


def _norm_mod(x, gain, shift, scale):
    r = lax.rsqrt(jnp.mean(x * x, axis=-1, keepdims=True) + RMS_EPS)
    return (x * r) * gain * (1.0 + scale) + shift


def _ada_kernel(c_ref, w_ref, b_ref, o_ref):
    c = c_ref[...]
    sc = c * jax.nn.sigmoid(c)
    o_ref[...] = jnp.dot(sc, w_ref[...], precision=lax.Precision.HIGHEST,
                         preferred_element_type=F32) + b_ref[...]


def ada_linear(c, w, b, *, tn=1024):
    bsz, d = c.shape
    n = w.shape[1]
    return pl.pallas_call(
        _ada_kernel,
        grid=(n // tn,),
        in_specs=[pl.BlockSpec((bsz, d), lambda j: (0, 0)),
                  pl.BlockSpec((d, tn), lambda j: (0, j)),
                  pl.BlockSpec((1, tn), lambda j: (0, j))],
        out_specs=pl.BlockSpec((bsz, tn), lambda j: (0, j)),
        out_shape=jax.ShapeDtypeStruct((bsz, n), F32),
        compiler_params=_cparams("arbitrary"),
        name="ada_linear",
    )(c, w, b.reshape(1, n))


def _mixer_a_kernel(x_ref, shift_ref, scale_ref, gate1_ref, gain_ref, win_ref, conv_ref, wout_ref,
                    o_ref, cu_ref, *, ts):
    d = D_MODEL
    x = x_ref[0]
    h = _norm_mod(x, gain_ref[...], shift_ref[0], scale_ref[0])
    proj = jnp.dot(h.astype(BF16), win_ref[...], preferred_element_type=F32)
    b_gate = proj[:, :d]
    cu = proj[:, d:2 * d] * proj[:, 2 * d:]

    @pl.when(pl.program_id(1) == 0)
    def _():
        cu_ref[0:SUBLANES, :] = jnp.zeros((SUBLANES, d), F32)

    cu_ref[SUBLANES:SUBLANES + ts, :] = cu
    w = conv_ref[...]
    z = (w[0:1] * cu_ref[SUBLANES - 2:SUBLANES - 2 + ts, :]
         + w[1:2] * cu_ref[SUBLANES - 1:SUBLANES - 1 + ts, :]
         + w[2:3] * cu)
    cu_ref[0:SUBLANES, :] = cu_ref[ts:ts + SUBLANES, :]
    y = jnp.dot((b_gate * z).astype(BF16), wout_ref[...], preferred_element_type=F32)
    o_ref[0] = x + gate1_ref[0] * y


def mixer_a(x, shift, scale, gate1, gain, w_in, conv_w, w_out, *, ts=256):
    bsz, s, d = x.shape
    vec = pl.BlockSpec((1, 1, d), lambda b, i: (b, 0, 0))
    return pl.pallas_call(
        functools.partial(_mixer_a_kernel, ts=ts),
        grid=(bsz, s // ts),
        in_specs=[pl.BlockSpec((1, ts, d), lambda b, i: (b, i, 0)), vec, vec, vec,
                  pl.BlockSpec((1, d), lambda b, i: (0, 0)),
                  pl.BlockSpec((d, 3 * d), lambda b, i: (0, 0)),
                  pl.BlockSpec((3, d), lambda b, i: (0, 0)),
                  pl.BlockSpec((d, d), lambda b, i: (0, 0))],
        out_specs=pl.BlockSpec((1, ts, d), lambda b, i: (b, i, 0)),
        out_shape=jax.ShapeDtypeStruct((bsz, s, d), F32),
        scratch_shapes=[pltpu.VMEM((ts + 2 * SUBLANES, d), F32)],
        compiler_params=_cparams("arbitrary", "arbitrary"),
        name="mixer_a",
    )(x, shift, scale, gate1, gain, w_in, conv_w, w_out)


def _topk_rows(s, k, payload=None):
    n = s.shape[0]
    rows = lax.broadcasted_iota(jnp.int32, s.shape, 0).astype(F32)
    vals, idxs, pays = [], [], []
    for _ in range(k):
        m = jnp.max(s, axis=0, keepdims=True)
        am = jnp.min(jnp.where(s == m, rows, float(n)), axis=0, keepdims=True)
        hit = rows == am
        vals.append(m)
        idxs.append(am)
        if payload is not None:
            pays.append(jnp.sum(jnp.where(hit, payload, 0), axis=0, keepdims=True))
        s = jnp.where(hit, -jnp.inf, s)
    return vals, idxs, pays


def _topk_rows_paired(s, k):
    n = s.shape[0]
    half = n // 2
    lo, hi = s[:half], s[half:]
    rows = lax.broadcasted_iota(jnp.int32, lo.shape, 0).astype(F32)
    swap = hi > lo
    a = jnp.where(swap, hi, lo)
    b = jnp.where(swap, lo, hi)
    ia = jnp.where(swap, rows + half, rows)
    ib = jnp.where(swap, rows, rows + half)
    vals, idxs = [], []
    for _ in range(k):
        m = jnp.max(a, axis=0, keepdims=True)
        am = jnp.min(jnp.where(a == m, ia, float(n)), axis=0, keepdims=True)
        hit = ia == am
        vals.append(m)
        idxs.append(am)
        a = jnp.where(hit, b, a)
        ia = jnp.where(hit, ib, ia)
        b = jnp.where(hit, -jnp.inf, b)
    return vals, idxs


def _router_kernel(x_ref, shift_ref, scale_ref, gain_ref, wq_ref, keys_ref, h_ref, idx_ref, gates_ref):
    x = x_ref[0]
    h = _norm_mod(x, gain_ref[...], shift_ref[0], scale_ref[0])
    hb = h.astype(BF16)
    h_ref[0] = hb
    q = jnp.dot(hb, wq_ref[...], preferred_element_type=F32).astype(BF16)
    idx_rows, gate_rows = [], []
    for head in range(PEER_HEADS):
        top_s, top_i = [], []
        for part in range(2):
            hp = head * 2 + part
            st = lax.dot_general(keys_ref[hp], q[:, hp * LANES:(hp + 1) * LANES], NT_DIMS,
                                 preferred_element_type=F32)
            vals, idxs = _topk_rows_paired(st, PEER_TOPK)
            top_s.append(vals)
            top_i.append(idxs)
        s2 = jnp.concatenate(top_s[1], axis=0)
        i2 = jnp.concatenate(top_i[1], axis=0).astype(jnp.int32)
        n_b = [PEER_TOPK // (a + 1) for a in range(PEER_TOPK)]
        pad = -(-sum(n_b) // SUBLANES) * SUBLANES - sum(n_b)
        cand_s = jnp.concatenate([top_s[0][a] + s2[:n_b[a]] for a in range(PEER_TOPK)]
                                 + [jnp.full((pad, s2.shape[1]), -jnp.inf, F32)], axis=0)
        cand_i = jnp.concatenate([top_i[0][a].astype(jnp.int32) * PEER_N_KEYS + i2[:n_b[a]]
                                  for a in range(PEER_TOPK)]
                                 + [jnp.zeros((pad, s2.shape[1]), jnp.int32)], axis=0)
        best_s, _, best_e = _topk_rows(cand_s, PEER_TOPK, payload=cand_i)
        bs = jnp.concatenate(best_s, axis=0)
        e = jnp.exp(bs - bs[0:1])
        gate_rows.append(e / jnp.sum(e, axis=0, keepdims=True))
        idx_rows.append(jnp.concatenate(best_e, axis=0))
    idx_ref[0] = (jnp.concatenate(idx_rows, axis=0) * ROWS_PER_EXPERT).T
    gates_ref[0] = jnp.concatenate(gate_rows, axis=0).T


def peer_router(x, shift, scale, gain, w_q, keys, *, tb=256):
    bsz, s, d = x.shape
    vec = pl.BlockSpec((1, 1, d), lambda b, i: (b, 0, 0))
    tok = lambda width: pl.BlockSpec((1, tb, width), lambda b, i: (b, i, 0))
    return pl.pallas_call(
        _router_kernel,
        grid=(bsz, s // tb),
        in_specs=[tok(d), vec, vec,
                  pl.BlockSpec((1, d), lambda b, i: (0, 0)),
                  pl.BlockSpec(w_q.shape, lambda b, i: (0, 0)),
                  pl.BlockSpec(keys.shape, lambda b, i: (0, 0, 0))],
        out_specs=[tok(d), tok(N_SEL), tok(N_SEL)],
        out_shape=[jax.ShapeDtypeStruct((bsz, s, d), BF16),
                   jax.ShapeDtypeStruct((bsz, s, N_SEL), jnp.int32),
                   jax.ShapeDtypeStruct((bsz, s, N_SEL), F32)],
        compiler_params=_cparams("arbitrary", "arbitrary"),
        name="peer_router",
    )(x, shift, scale, gain, w_q, keys)


def pack_table(tab):
    bits = lax.bitcast_convert_type(tab, jnp.uint32)
    bf16_bits = (bits + jnp.uint32(0x7FFF) + ((bits >> 16) & jnp.uint32(1))) >> 16
    words = bf16_bits[..., :HALF_D] | (bf16_bits[..., HALF_D:] << 16)
    return lax.bitcast_convert_type(words, jnp.int32).reshape(*tab.shape[:-2], -1, LANES)


def _unpack(words):
    lo = pltpu.bitcast(words << 16, F32)
    hi = pltpu.bitcast(words & jnp.int32(-65536), F32)
    return lo, hi


def _gather_tile(idx_refs, tok, tab_ref, g_ref, slot):
    per = N_SEL // IDX_SPLIT
    base = tok * per
    for m in range(per):
        for a in range(IDX_SPLIT):
            k = a * per + m
            e4 = pl.multiple_of(idx_refs[a][base + m], ROWS_PER_EXPERT)
            g_ref[slot, pl.ds(k, ROWS_PER_EXPERT, stride=GATHER_STRIDE), :] = tab_ref[pl.ds(e4, ROWS_PER_EXPERT), :]


def _tile_halves(g_ref, slot, c):
    lo, hi = _unpack(g_ref[slot, pl.ds(c * GATHER_STRIDE, N_SEL), :])
    return lo.astype(BF16), hi.astype(BF16)


def _peer_u_kernel(*refs, tb):
    idx_refs = refs[:IDX_SPLIT]
    h_ref, gate_ref, tab_ref, w_ref, g_ref = refs[IDX_SPLIT:]

    def group(g, sub):
        r0 = pl.multiple_of((g * SUBGROUPS + sub) * SUBLANES, SUBLANES)
        x8 = h_ref[pl.ds(r0, SUBLANES), :]
        row = lax.broadcasted_iota(jnp.int32, (SUBLANES, N_SEL), 0)
        acc = jnp.zeros((SUBLANES, N_SEL), F32)
        for j in range(SUBLANES):
            slot = sub * SUBLANES + j
            _gather_tile(idx_refs, r0 + j, tab_ref, g_ref, slot)
            res = jnp.zeros((SUBLANES, N_SEL), F32)
            for c in range(ROWS_PER_EXPERT):
                lo, hi = _tile_halves(g_ref, slot, c)
                rows = jnp.concatenate([lo, hi], axis=1)
                xs = jnp.concatenate([x8[:, c * LANES:(c + 1) * LANES],
                                      x8[:, HALF_D + c * LANES:HALF_D + (c + 1) * LANES]], axis=1)
                res += lax.dot_general(xs, rows, NT_DIMS, preferred_element_type=F32)
            acc = jnp.where(row == j, res, acc)
        w_ref[pl.ds(r0, SUBLANES), :] = gate_ref[pl.ds(r0, SUBLANES), :] * jax.nn.gelu(acc)

    def body(g, carry):
        for sub in range(SUBGROUPS):
            group(g, sub)
        return carry

    lax.fori_loop(0, tb // (SUBLANES * SUBGROUPS), body, 0)


def _idx_specs(tb):
    return [pl.BlockSpec((tb * (N_SEL // IDX_SPLIT),), lambda i: (i,), memory_space=pltpu.SMEM)] * IDX_SPLIT


def peer_u_phase(idx_parts, h, gates, utab, *, tb=64):
    t = h.shape[0]
    return pl.pallas_call(
        functools.partial(_peer_u_kernel, tb=tb),
        grid=(t // tb,),
        in_specs=_idx_specs(tb) + [
                  pl.BlockSpec((tb, D_MODEL), lambda i: (i, 0)),
                  pl.BlockSpec((tb, N_SEL), lambda i: (i, 0)),
                  pl.BlockSpec(memory_space=pltpu.VMEM)],
        out_specs=pl.BlockSpec((tb, N_SEL), lambda i: (i, 0)),
        out_shape=jax.ShapeDtypeStruct((t, N_SEL), F32),
        scratch_shapes=[pltpu.VMEM((SUBLANES * SUBGROUPS, ROWS_PER_EXPERT * GATHER_STRIDE, LANES), jnp.int32)],
        compiler_params=_cparams("arbitrary"),
        name="peer_u",
    )(*idx_parts, h, gates, utab)


def _peer_v_kernel(*refs, tb):
    idx_refs = refs[:IDX_SPLIT]
    w_ref, x_ref, gate1_ref, tab_ref, o_ref, g_ref = refs[IDX_SPLIT:]

    def group(g, sub):
        r0 = pl.multiple_of((g * SUBGROUPS + sub) * SUBLANES, SUBLANES)
        w8 = w_ref[pl.ds(r0, SUBLANES), :]
        w_hi = w8.astype(BF16)
        w_lo = (w8 - w_hi.astype(F32)).astype(BF16)
        row = lax.broadcasted_iota(jnp.int32, (SUBLANES, N_SEL), 0)
        zero = jnp.zeros_like(w_hi)
        y = jnp.zeros((2 * SUBLANES, D_MODEL), F32)
        for j in range(SUBLANES):
            slot = sub * SUBLANES + j
            _gather_tile(idx_refs, r0 + j, tab_ref, g_ref, slot)
            wm = jnp.concatenate([jnp.where(row == j, w_hi, zero), jnp.where(row == j, w_lo, zero)], axis=0)
            halves = [_tile_halves(g_ref, slot, c) for c in range(ROWS_PER_EXPERT)]
            vals = jnp.concatenate([h[0] for h in halves] + [h[1] for h in halves], axis=1)
            y = y + jnp.dot(wm, vals, preferred_element_type=F32)
        o_ref[pl.ds(r0, SUBLANES), :] = x_ref[pl.ds(r0, SUBLANES), :] + gate1_ref[0] * (y[:SUBLANES] + y[SUBLANES:])

    def body(g, carry):
        for sub in range(SUBGROUPS):
            group(g, sub)
        return carry

    lax.fori_loop(0, tb // (SUBLANES * SUBGROUPS), body, 0)


def peer_v_phase(idx_parts, w, x, gate1, vtab, *, tb=64):
    t = x.shape[0]
    blocks_per_batch = (t // gate1.shape[0]) // tb
    return pl.pallas_call(
        functools.partial(_peer_v_kernel, tb=tb),
        grid=(t // tb,),
        in_specs=_idx_specs(tb) + [
                  pl.BlockSpec((tb, N_SEL), lambda i: (i, 0)),
                  pl.BlockSpec((tb, D_MODEL), lambda i: (i, 0)),
                  pl.BlockSpec((1, 1, D_MODEL), lambda i: (i // blocks_per_batch, 0, 0)),
                  pl.BlockSpec(memory_space=pltpu.VMEM)],
        out_specs=pl.BlockSpec((tb, D_MODEL), lambda i: (i, 0)),
        out_shape=jax.ShapeDtypeStruct((t, D_MODEL), F32),
        scratch_shapes=[pltpu.VMEM((SUBLANES * SUBGROUPS, ROWS_PER_EXPERT * GATHER_STRIDE, LANES), jnp.int32)],
        compiler_params=_cparams("arbitrary"),
        name="peer_v",
    )(*idx_parts, w, x, gate1, vtab)


def peer_layer(x, shift, scale, gate1, gain, w_q, keys, utab, vtab):
    bsz, s, d = x.shape
    t = bsz * s
    h, idx, gates = peer_router(x, shift, scale, gain, w_q, keys)
    idx = idx.reshape(t, IDX_SPLIT, N_SEL // IDX_SPLIT)
    idx_parts = [idx[:, a, :].reshape(-1) for a in range(IDX_SPLIT)]
    w = peer_u_phase(idx_parts, h.reshape(t, d), gates.reshape(t, N_SEL), utab)
    out = peer_v_phase(idx_parts, w, x.reshape(t, d), gate1, vtab)
    return out.reshape(bsz, s, d)


def _split3(x):
    a = x.astype(BF16)
    r = x - a.astype(F32)
    b = r.astype(BF16)
    c = (r - b.astype(F32)).astype(BF16)
    return a, b, c


def _kvq_kernel(x_ref, kshift_ref, kscale_ref, kgain_ref, qshift_ref, qscale_ref, qgain_ref,
                wkv_ref, wf_hi_ref, wf_lo_ref, fb_ref, wq_ref,
                q_ref, k_ref, v_ref, cf_ref, carry_ref, *, ts):
    d = D_MODEL
    x = x_ref[0]
    r = lax.rsqrt(jnp.mean(x * x, axis=-1, keepdims=True) + RMS_EPS)
    xn = x * r
    hk = xn * kgain_ref[...] * (1.0 + kscale_ref[0]) + kshift_ref[0]
    hq = xn * qgain_ref[...] * (1.0 + qscale_ref[0]) + qshift_ref[0]
    hk_hi = hk.astype(BF16)
    kv = jnp.dot(hk_hi, wkv_ref[...], preferred_element_type=F32)
    k_ref[0] = kv[:, :d].astype(BF16)
    v_ref[0] = kv[:, d:].astype(BF16)
    q = jnp.dot(hq.astype(BF16), wq_ref[...], preferred_element_type=F32)
    q_ref[0] = (q * (LOG2_E * HEAD_DIM ** -0.5)).astype(BF16)

    hk_lo = (hk - hk_hi.astype(F32)).astype(BF16)
    f_logit = (jnp.dot(hk_hi, wf_hi_ref[...], preferred_element_type=F32)
               + jnp.dot(hk_lo, wf_hi_ref[...], preferred_element_type=F32)
               + jnp.dot(hk_hi, wf_lo_ref[...], preferred_element_type=F32))
    z = f_logit + fb_ref[...]
    log_f = LOG2_E * (jnp.minimum(z, 0.0) - jnp.log1p(jnp.exp(-jnp.abs(z))))

    @pl.when(pl.program_id(1) == 0)
    def _():
        carry_ref[...] = jnp.zeros_like(carry_ref)

    tri = (lax.broadcasted_iota(jnp.int32, (ts, ts), 0)
           >= lax.broadcasted_iota(jnp.int32, (ts, ts), 1)).astype(BF16)
    cum = carry_ref[...]
    for piece in _split3(log_f):
        cum = cum + jnp.dot(tri, piece, preferred_element_type=F32)
    cf_ref[0] = cum
    carry_ref[...] = cum[ts - 1:ts, :]


def kvq_proj(x, kshift, kscale, kgain, qshift, qscale, qgain, w_kv, wf_hi, wf_lo, f_bias, w_q, *, ts=512):
    bsz, s, d = x.shape
    vec = pl.BlockSpec((1, 1, d), lambda b, i: (b, 0, 0))
    const = lambda a: pl.BlockSpec(a.shape, lambda b, i: (0,) * a.ndim)
    tok = lambda width: pl.BlockSpec((1, ts, width), lambda b, i: (b, i, 0))
    return pl.pallas_call(
        functools.partial(_kvq_kernel, ts=ts),
        grid=(bsz, s // ts),
        in_specs=[tok(d), vec, vec, const(kgain), vec, vec, const(qgain),
                  const(w_kv), const(wf_hi), const(wf_lo), const(f_bias), const(w_q)],
        out_specs=[tok(d), tok(d), tok(d), tok(LANES)],
        out_shape=[jax.ShapeDtypeStruct((bsz, s, d), BF16)] * 3
        + [jax.ShapeDtypeStruct((bsz, s, LANES), F32)],
        scratch_shapes=[pltpu.VMEM((1, LANES), F32)],
        compiler_params=_cparams("arbitrary", "arbitrary"),
        name="kvq_proj",
    )(x, kshift, kscale, kgain, qshift, qscale, qgain, w_kv, wf_hi, wf_lo, f_bias, w_q)


def _fox_kernel(q_ref, k_ref, v_ref, cfq_ref, cfk_ref, o_ref, q2_ref, cq_ref, m_ref, l_ref, acc_ref, *, tq):
    pair = pl.program_id(1)
    qi = pl.program_id(2)
    reps = tq // LANES
    lane = lax.broadcasted_iota(jnp.int32, (tq, LANES), 1)
    first_head = lane < HEAD_DIM
    q = q_ref[0]
    zq = jnp.zeros_like(q)
    q2_ref[0:tq, :] = jnp.where(first_head, q, zq)
    q2_ref[tq:2 * tq, :] = jnp.where(first_head, zq, q)
    cf = cfq_ref[0]
    for sub in range(2):
        col = jnp.sum(jnp.where(lane == 2 * pair + sub, cf, 0.0), axis=1, keepdims=True)
        cq_ref[sub * tq:(sub + 1) * tq, :] = jnp.broadcast_to(col, (tq, LANES))
    m_ref[...] = jnp.full_like(m_ref, -jnp.inf)
    l_ref[...] = jnp.zeros_like(l_ref)
    acc_ref[...] = jnp.zeros_like(acc_ref)
    below_diag = (lax.broadcasted_iota(jnp.int32, (tq, tq), 1)
                  <= lax.broadcasted_iota(jnp.int32, (tq, tq), 0))

    def block(j, masked):
        k0 = pl.multiple_of(j * tq, tq)
        kb = k_ref[0, pl.ds(k0, tq), :]
        vb = v_ref[0, pl.ds(k0, tq), :]
        s = lax.dot_general(q2_ref[...], kb, NT_DIMS, preferred_element_type=F32)
        ck = cfk_ref[0, 0, :, pl.ds(k0, tq)]
        ps, alphas = [], []
        for sub in range(2):
            rows = slice(sub * tq, (sub + 1) * tq)
            ss = s[rows] + pltpu.repeat(cq_ref[rows, :], reps, axis=1) - ck[sub:sub + 1, :]
            if masked:
                ss = jnp.where(below_diag, ss, -jnp.inf)
            m_prev = m_ref[rows, :]
            m_new = jnp.maximum(m_prev, jnp.max(ss, axis=1, keepdims=True))
            alpha = jnp.exp2(m_prev - m_new)
            p = jnp.exp2(ss - pltpu.repeat(m_new, reps, axis=1))
            l_ref[rows, :] = alpha * l_ref[rows, :] + jnp.sum(p, axis=1, keepdims=True)
            m_ref[rows, :] = m_new
            ps.append(p.astype(BF16))
            alphas.append(alpha)
        pv = jnp.dot(jnp.concatenate(ps, axis=0), vb, preferred_element_type=F32)
        acc_ref[...] = jnp.concatenate(alphas, axis=0) * acc_ref[...] + pv

    def full_block(j, carry):
        block(j, False)
        return carry

    lax.fori_loop(0, qi, full_block, 0)
    block(qi, True)
    o = jnp.where(first_head, acc_ref[0:tq, :] / l_ref[0:tq, :], acc_ref[tq:2 * tq, :] / l_ref[tq:2 * tq, :])
    o_ref[0] = o.astype(o_ref.dtype)


def fox_attention(q, k, v, cf_tok, cf_head, *, tq=512):
    bsz, s, d = q.shape
    n_pairs = N_HEADS // 2
    return pl.pallas_call(
        functools.partial(_fox_kernel, tq=tq),
        grid=(bsz, n_pairs, s // tq),
        in_specs=[pl.BlockSpec((1, tq, LANES), lambda b, p, i: (b, i, p)),
                  pl.BlockSpec((1, s, LANES), lambda b, p, i: (b, 0, p)),
                  pl.BlockSpec((1, s, LANES), lambda b, p, i: (b, 0, p)),
                  pl.BlockSpec((1, tq, LANES), lambda b, p, i: (b, i, 0)),
                  pl.BlockSpec((1, 1, 2, s), lambda b, p, i: (b, p, 0, 0))],
        out_specs=pl.BlockSpec((1, tq, LANES), lambda b, p, i: (b, i, p)),
        out_shape=jax.ShapeDtypeStruct((bsz, s, d), BF16),
        scratch_shapes=[pltpu.VMEM((2 * tq, LANES), BF16),
                        pltpu.VMEM((2 * tq, LANES), F32),
                        pltpu.VMEM((2 * tq, LANES), F32),
                        pltpu.VMEM((2 * tq, LANES), F32),
                        pltpu.VMEM((2 * tq, LANES), F32)],
        compiler_params=_cparams("arbitrary", "arbitrary", "arbitrary"),
        name="fox_attention",
    )(q, k, v, cf_tok, cf_head)


def _out_proj_kernel(a_ref, x_ref, gate1_ref, w_ref, o_ref):
    y = jnp.dot(a_ref[0], w_ref[...], preferred_element_type=F32)
    o_ref[0] = x_ref[0] + gate1_ref[0] * y


def out_proj(a, x, gate1, w, *, ts=512):
    bsz, s, d = x.shape
    tok = pl.BlockSpec((1, ts, d), lambda b, i: (b, i, 0))
    return pl.pallas_call(
        _out_proj_kernel,
        grid=(bsz, s // ts),
        in_specs=[tok, tok, pl.BlockSpec((1, 1, d), lambda b, i: (b, 0, 0)),
                  pl.BlockSpec((d, d), lambda b, i: (0, 0))],
        out_specs=tok,
        out_shape=jax.ShapeDtypeStruct((bsz, s, d), F32),
        compiler_params=_cparams("arbitrary", "arbitrary"),
        name="out_proj",
    )(a, x, gate1, w)


def _final_norm_kernel(x_ref, gain_ref, o_ref):
    x = x_ref[...]
    r = lax.rsqrt(jnp.mean(x * x, axis=-1, keepdims=True) + RMS_EPS)
    o_ref[...] = (x * r) * gain_ref[...]


def final_norm_call(x, gain, *, ts=1024):
    t, d = x.shape
    return pl.pallas_call(
        _final_norm_kernel,
        grid=(t // ts,),
        in_specs=[pl.BlockSpec((ts, d), lambda i: (i, 0)), pl.BlockSpec((1, d), lambda i: (0, 0))],
        out_specs=pl.BlockSpec((ts, d), lambda i: (i, 0)),
        out_shape=jax.ShapeDtypeStruct((t, d), F32),
        compiler_params=_cparams("arbitrary"),
        name="final_norm",
    )(x, gain)


def kernel(x, c, mix_norm, mix_ada_w, mix_ada_b, ffn_norm, ffn_ada_w, ffn_ada_b, a_w_in, a_conv, a_w_out, kv_norm, kv_ada_w, kv_ada_b, kv_w, f_bias, b_w_q, b_w_o, peer_w_q, peer_sub_keys, peer_u, peer_v, final_norm):
    bsz, s, d = x.shape
    depth = mix_norm.shape[0]
    n_a = a_w_in.shape[0]

    ada_w = jnp.concatenate([mix_ada_w[l] for l in range(depth)] + [ffn_ada_w[l] for l in range(depth)]
                            + [kv_ada_w], axis=1)
    ada_b = jnp.concatenate([mix_ada_b[l] for l in range(depth)] + [ffn_ada_b[l] for l in range(depth)]
                            + [kv_ada_b], axis=0)
    ada = ada_linear(c, ada_w, ada_b)

    def ada_chunk(i):
        return ada[:, i * d:(i + 1) * d].reshape(bsz, 1, d)

    def mix_ada(l):
        return ada_chunk(3 * l), ada_chunk(3 * l + 1), 1.0 + ada_chunk(3 * l + 2)

    def ffn_ada(l):
        o = 3 * depth + 3 * l
        return ada_chunk(o), ada_chunk(o + 1), 1.0 + ada_chunk(o + 2)

    kv_shift, kv_scale = ada_chunk(6 * depth), ada_chunk(6 * depth + 1)

    u_tabs = pack_table(peer_u)
    v_tabs = pack_table(peer_v)
    kv = None
    for layer in range(depth):
        shift, scale, gate1 = mix_ada(layer)
        gain = mix_norm[layer].reshape(1, d)
        if layer < n_a:
            x = mixer_a(x, shift, scale, gate1, gain, a_w_in[layer].astype(BF16), a_conv[layer],
                        a_w_out[layer].astype(BF16))
        else:
            j = layer - n_a
            if j == 0:
                w_f = jnp.pad(kv_w[:, 2 * d:], ((0, 0), (0, LANES - N_HEADS)))
                wf_hi = w_f.astype(BF16)
                wf_lo = (w_f - wf_hi.astype(F32)).astype(BF16)
                fb = jnp.pad(f_bias, (0, LANES - N_HEADS)).reshape(1, LANES)
                q, k, v, cf = kvq_proj(x, kv_shift, kv_scale, kv_norm.reshape(1, d), shift, scale, gain,
                                       kv_w[:, :2 * d].astype(BF16), wf_hi, wf_lo, fb, b_w_q[j].astype(BF16))
                kv = (k, v, cf, jnp.swapaxes(cf[:, :, :N_HEADS], 1, 2).reshape(bsz, N_HEADS // 2, 2, s))
            else:
                raise NotImplementedError("more than one attention layer")
            attn = fox_attention(q, kv[0], kv[1], kv[2], kv[3])
            x = out_proj(attn, x, gate1, b_w_o[j].astype(BF16))
        shift, scale, gate1 = ffn_ada(layer)
        keys = peer_sub_keys[layer].reshape(2 * PEER_HEADS, PEER_N_KEYS, -1).astype(BF16)
        x = peer_layer(x, shift, scale, gate1, ffn_norm[layer].reshape(1, d), peer_w_q[layer].astype(BF16),
                       keys, u_tabs[layer], v_tabs[layer])
    return final_norm_call(x.reshape(bsz * s, d), final_norm.reshape(1, d)).reshape(bsz, s, d)
```

```python
import functools

import jax
import jax.numpy as jnp
from jax import lax
from jax.experimental import pallas as pl
from jax.experimental.pallas import tpu as pltpu

D_MODEL = 1024
N_HEADS = 16
HEAD_DIM = 64
PEER_HEADS = 8
PEER_N_KEYS = 128
PEER_TOPK = 16
N_SEL = PEER_HEADS * PEER_TOPK
RMS_EPS = 1e-6
LOG2_E = 1.4426950408889634

LANES = 128
SUBLANES = 8
HALF_D = D_MODEL // 2
ROWS_PER_EXPERT = HALF_D // LANES
IDX_SPLIT = 8
SUBGROUPS = 16
GATHER_SLOTS = 32
GATHER_STRIDE = 136
VMEM_LIMIT = 56 * 1024 * 1024

BF16 = jnp.bfloat16
F32 = jnp.float32
NT_DIMS = (((1,), (1,)), ((), ()))


def _cparams(*sem):
    return pltpu.CompilerParams(dimension_semantics=sem, vmem_limit_bytes=VMEM_LIMIT)


def _norm_mod(x, gain, shift, scale):
    r = lax.rsqrt(jnp.mean(x * x, axis=-1, keepdims=True) + RMS_EPS)
    return (x * r) * gain * (1.0 + scale) + shift


def _ada_kernel(c_ref, w_ref, b_ref, o_ref):
    c = c_ref[...]
    sc = c * jax.nn.sigmoid(c)
    o_ref[...] = jnp.dot(sc, w_ref[...], precision=lax.Precision.HIGHEST,
                         preferred_element_type=F32) + b_ref[...]


def ada_linear(c, w, b, *, tn=1024):
    bsz, d = c.shape
    n = w.shape[1]
    return pl.pallas_call(
        _ada_kernel,
        grid=(n // tn,),
        in_specs=[pl.BlockSpec((bsz, d), lambda j: (0, 0)),
                  pl.BlockSpec((d, tn), lambda j: (0, j)),
                  pl.BlockSpec((1, tn), lambda j: (0, j))],
        out_specs=pl.BlockSpec((bsz, tn), lambda j: (0, j)),
        out_shape=jax.ShapeDtypeStruct((bsz, n), F32),
        compiler_params=_cparams("arbitrary"),
        name="ada_linear",
    )(c, w, b.reshape(1, n))


def _mixer_a_kernel(x_ref, shift_ref, scale_ref, gate1_ref, gain_ref, win_ref, conv_ref, wout_ref,
                    o_ref, cu_ref, *, ts):
    d = D_MODEL
    x = x_ref[0]
    h = _norm_mod(x, gain_ref[...], shift_ref[0], scale_ref[0])
    proj = jnp.dot(h.astype(BF16), win_ref[...], preferred_element_type=F32)
    b_gate = proj[:, :d]
    cu = proj[:, d:2 * d] * proj[:, 2 * d:]

    @pl.when(pl.program_id(1) == 0)
    def _():
        cu_ref[0:SUBLANES, :] = jnp.zeros((SUBLANES, d), F32)

    cu_ref[SUBLANES:SUBLANES + ts, :] = cu
    w = conv_ref[...]
    z = (w[0:1] * cu_ref[SUBLANES - 2:SUBLANES - 2 + ts, :]
         + w[1:2] * cu_ref[SUBLANES - 1:SUBLANES - 1 + ts, :]
         + w[2:3] * cu)
    cu_ref[0:SUBLANES, :] = cu_ref[ts:ts + SUBLANES, :]
    y = jnp.dot((b_gate * z).astype(BF16), wout_ref[...], preferred_element_type=F32)
    o_ref[0] = x + gate1_ref[0] * y


def mixer_a(x, shift, scale, gate1, gain, w_in, conv_w, w_out, *, ts=256):
    bsz, s, d = x.shape
    vec = pl.BlockSpec((1, 1, d), lambda b, i: (b, 0, 0))
    return pl.pallas_call(
        functools.partial(_mixer_a_kernel, ts=ts),
        grid=(bsz, s // ts),
        in_specs=[pl.BlockSpec((1, ts, d), lambda b, i: (b, i, 0)), vec, vec, vec,
                  pl.BlockSpec((1, d), lambda b, i: (0, 0)),
                  pl.BlockSpec((d, 3 * d), lambda b, i: (0, 0)),
                  pl.BlockSpec((3, d), lambda b, i: (0, 0)),
                  pl.BlockSpec((d, d), lambda b, i: (0, 0))],
        out_specs=pl.BlockSpec((1, ts, d), lambda b, i: (b, i, 0)),
        out_shape=jax.ShapeDtypeStruct((bsz, s, d), F32),
        scratch_shapes=[pltpu.VMEM((ts + 2 * SUBLANES, d), F32)],
        compiler_params=_cparams("arbitrary", "arbitrary"),
        name="mixer_a",
    )(x, shift, scale, gate1, gain, w_in, conv_w, w_out)


def _topk_rows(s, k, payload=None):
    n = s.shape[0]
    rows = lax.broadcasted_iota(jnp.int32, s.shape, 0).astype(F32)
    vals, idxs, pays = [], [], []
    for _ in range(k):
        m = jnp.max(s, axis=0, keepdims=True)
        am = jnp.min(jnp.where(s == m, rows, float(n)), axis=0, keepdims=True)
        hit = rows == am
        vals.append(m)
        idxs.append(am)
        if payload is not None:
            pays.append(jnp.sum(jnp.where(hit, payload, 0), axis=0, keepdims=True))
        s = jnp.where(hit, -jnp.inf, s)
    return vals, idxs, pays


def _topk_rows_paired(s, k):
    n = s.shape[0]
    half = n // 2
    lo, hi = s[:half], s[half:]
    rows = lax.broadcasted_iota(jnp.int32, lo.shape, 0).astype(F32)
    swap = hi > lo
    a = jnp.where(swap, hi, lo)
    b = jnp.where(swap, lo, hi)
    ia = jnp.where(swap, rows + half, rows)
    ib = jnp.where(swap, rows, rows + half)
    vals, idxs = [], []
    for _ in range(k):
        m = jnp.max(a, axis=0, keepdims=True)
        am = jnp.min(jnp.where(a == m, ia, float(n)), axis=0, keepdims=True)
        hit = ia == am
        vals.append(m)
        idxs.append(am)
        a = jnp.where(hit, b, a)
        ia = jnp.where(hit, ib, ia)
        b = jnp.where(hit, -jnp.inf, b)
    return vals, idxs


def _router_kernel(x_ref, shift_ref, scale_ref, gain_ref, wq_ref, keys_ref, h_ref, idx_ref, gates_ref):
    x = x_ref[0]
    h = _norm_mod(x, gain_ref[...], shift_ref[0], scale_ref[0])
    hb = h.astype(BF16)
    h_ref[0] = hb
    q = jnp.dot(hb, wq_ref[...], preferred_element_type=F32).astype(BF16)
    idx_rows, gate_rows = [], []
    for head in range(PEER_HEADS):
        top_s, top_i = [], []
        for part in range(2):
            hp = head * 2 + part
            st = lax.dot_general(keys_ref[hp], q[:, hp * LANES:(hp + 1) * LANES], NT_DIMS,
                                 preferred_element_type=F32)
            vals, idxs = _topk_rows_paired(st, PEER_TOPK)
            top_s.append(vals)
            top_i.append(idxs)
        s2 = jnp.concatenate(top_s[1], axis=0)
        i2 = jnp.concatenate(top_i[1], axis=0).astype(jnp.int32)
        n_b = [PEER_TOPK // (a + 1) for a in range(PEER_TOPK)]
        pad = -(-sum(n_b) // SUBLANES) * SUBLANES - sum(n_b)
        cand_s = jnp.concatenate([top_s[0][a] + s2[:n_b[a]] for a in range(PEER_TOPK)]
                                 + [jnp.full((pad, s2.shape[1]), -jnp.inf, F32)], axis=0)
        cand_i = jnp.concatenate([top_i[0][a].astype(jnp.int32) * PEER_N_KEYS + i2[:n_b[a]]
                                  for a in range(PEER_TOPK)]
                                 + [jnp.zeros((pad, s2.shape[1]), jnp.int32)], axis=0)
        best_s, _, best_e = _topk_rows(cand_s, PEER_TOPK, payload=cand_i)
        bs = jnp.concatenate(best_s, axis=0)
        e = jnp.exp(bs - bs[0:1])
        gate_rows.append(e / jnp.sum(e, axis=0, keepdims=True))
        idx_rows.append(jnp.concatenate(best_e, axis=0))
    idx_ref[0] = (jnp.concatenate(idx_rows, axis=0) * ROWS_PER_EXPERT).T
    gates_ref[0] = jnp.concatenate(gate_rows, axis=0).T


def peer_router(x, shift, scale, gain, w_q, keys, *, tb=256):
    bsz, s, d = x.shape
    vec = pl.BlockSpec((1, 1, d), lambda b, i: (b, 0, 0))
    tok = lambda width: pl.BlockSpec((1, tb, width), lambda b, i: (b, i, 0))
    return pl.pallas_call(
        _router_kernel,
        grid=(bsz, s // tb),
        in_specs=[tok(d), vec, vec,
                  pl.BlockSpec((1, d), lambda b, i: (0, 0)),
                  pl.BlockSpec(w_q.shape, lambda b, i: (0, 0)),
                  pl.BlockSpec(keys.shape, lambda b, i: (0, 0, 0))],
        out_specs=[tok(d), tok(N_SEL), tok(N_SEL)],
        out_shape=[jax.ShapeDtypeStruct((bsz, s, d), BF16),
                   jax.ShapeDtypeStruct((bsz, s, N_SEL), jnp.int32),
                   jax.ShapeDtypeStruct((bsz, s, N_SEL), F32)],
        compiler_params=_cparams("arbitrary", "arbitrary"),
        name="peer_router",
    )(x, shift, scale, gain, w_q, keys)


def pack_table(tab):
    bits = lax.bitcast_convert_type(tab, jnp.uint32)
    bf16_bits = (bits + jnp.uint32(0x7FFF) + ((bits >> 16) & jnp.uint32(1))) >> 16
    words = bf16_bits[..., :HALF_D] | (bf16_bits[..., HALF_D:] << 16)
    return lax.bitcast_convert_type(words, jnp.int32).reshape(*tab.shape[:-2], -1, LANES)


def _unpack(words):
    lo = pltpu.bitcast(words << 16, F32)
    hi = pltpu.bitcast(words & jnp.int32(-65536), F32)
    return lo, hi


def _gather_tile(idx_refs, tok, tab_ref, g_ref, slot):
    per = N_SEL // IDX_SPLIT
    base = tok * per
    for m in range(per):
        for a in range(IDX_SPLIT):
            k = a * per + m
            e4 = pl.multiple_of(idx_refs[a][base + m], ROWS_PER_EXPERT)
            g_ref[slot, pl.ds(k, ROWS_PER_EXPERT, stride=GATHER_STRIDE), :] = tab_ref[pl.ds(e4, ROWS_PER_EXPERT), :]


def _tile_halves(g_ref, slot, c):
    lo, hi = _unpack(g_ref[slot, pl.ds(c * GATHER_STRIDE, N_SEL), :])
    return lo.astype(BF16), hi.astype(BF16)


def _peer_u_kernel(*refs, tb):
    idx_refs = refs[:IDX_SPLIT]
    h_ref, gate_ref, tab_ref, w_ref, g_ref = refs[IDX_SPLIT:]

    def group(g, sub):
        r0 = pl.multiple_of((g * SUBGROUPS + sub) * SUBLANES, SUBLANES)
        x8 = h_ref[pl.ds(r0, SUBLANES), :]
        row = lax.broadcasted_iota(jnp.int32, (SUBLANES, N_SEL), 0)
        acc = jnp.zeros((SUBLANES, N_SEL), F32)
        for j in range(SUBLANES):
            slot = (sub * SUBLANES + j) % GATHER_SLOTS
            _gather_tile(idx_refs, r0 + j, tab_ref, g_ref, slot)
            res = jnp.zeros((SUBLANES, N_SEL), F32)
            for c in range(ROWS_PER_EXPERT):
                lo, hi = _tile_halves(g_ref, slot, c)
                rows = jnp.concatenate([lo, hi], axis=1)
                xs = jnp.concatenate([x8[:, c * LANES:(c + 1) * LANES],
                                      x8[:, HALF_D + c * LANES:HALF_D + (c + 1) * LANES]], axis=1)
                res += lax.dot_general(xs, rows, NT_DIMS, preferred_element_type=F32)
            acc = jnp.where(row == j, res, acc)
        w_ref[pl.ds(r0, SUBLANES), :] = gate_ref[pl.ds(r0, SUBLANES), :] * jax.nn.gelu(acc)

    def body(g, carry):
        for sub in range(SUBGROUPS):
            group(g, sub)
        return carry

    lax.fori_loop(0, tb // (SUBLANES * SUBGROUPS), body, 0)


def _idx_specs(tb):
    return [pl.BlockSpec((tb * (N_SEL // IDX_SPLIT),), lambda i: (i,), memory_space=pltpu.SMEM)] * IDX_SPLIT


def peer_u_phase(idx_parts, h, gates, utab, *, tb=SUBLANES * SUBGROUPS):
    t = h.shape[0]
    return pl.pallas_call(
        functools.partial(_peer_u_kernel, tb=tb),
        grid=(t // tb,),
        in_specs=_idx_specs(tb) + [
                  pl.BlockSpec((tb, D_MODEL), lambda i: (i, 0)),
                  pl.BlockSpec((tb, N_SEL), lambda i: (i, 0)),
                  pl.BlockSpec(memory_space=pltpu.VMEM)],
        out_specs=pl.BlockSpec((tb, N_SEL), lambda i: (i, 0)),
        out_shape=jax.ShapeDtypeStruct((t, N_SEL), F32),
        scratch_shapes=[pltpu.VMEM((GATHER_SLOTS, ROWS_PER_EXPERT * GATHER_STRIDE, LANES), jnp.int32)],
        compiler_params=_cparams("arbitrary"),
        name="peer_u",
    )(*idx_parts, h, gates, utab)


def _peer_v_kernel(*refs, tb):
    idx_refs = refs[:IDX_SPLIT]
    w_ref, x_ref, gate1_ref, tab_ref, o_ref, g_ref = refs[IDX_SPLIT:]

    def group(g, sub):
        r0 = pl.multiple_of((g * SUBGROUPS + sub) * SUBLANES, SUBLANES)
        w8 = w_ref[pl.ds(r0, SUBLANES), :]
        w_hi = w8.astype(BF16)
        w_lo = (w8 - w_hi.astype(F32)).astype(BF16)
        row = lax.broadcasted_iota(jnp.int32, (SUBLANES, N_SEL), 0)
        zero = jnp.zeros_like(w_hi)
        y = jnp.zeros((2 * SUBLANES, D_MODEL), F32)
        for j in range(SUBLANES):
            slot = (sub * SUBLANES + j) % GATHER_SLOTS
            _gather_tile(idx_refs, r0 + j, tab_ref, g_ref, slot)
            wm = jnp.concatenate([jnp.where(row == j, w_hi, zero), jnp.where(row == j, w_lo, zero)], axis=0)
            halves = [_tile_halves(g_ref, slot, c) for c in range(ROWS_PER_EXPERT)]
            vals = jnp.concatenate([h[0] for h in halves] + [h[1] for h in halves], axis=1)
            y = y + jnp.dot(wm, vals, preferred_element_type=F32)
        o_ref[pl.ds(r0, SUBLANES), :] = x_ref[pl.ds(r0, SUBLANES), :] + gate1_ref[0] * (y[:SUBLANES] + y[SUBLANES:])

    def body(g, carry):
        for sub in range(SUBGROUPS):
            group(g, sub)
        return carry

    lax.fori_loop(0, tb // (SUBLANES * SUBGROUPS), body, 0)


def peer_v_phase(idx_parts, w, x, gate1, vtab, *, tb=SUBLANES * SUBGROUPS):
    t = x.shape[0]
    blocks_per_batch = (t // gate1.shape[0]) // tb
    return pl.pallas_call(
        functools.partial(_peer_v_kernel, tb=tb),
        grid=(t // tb,),
        in_specs=_idx_specs(tb) + [
                  pl.BlockSpec((tb, N_SEL), lambda i: (i, 0)),
                  pl.BlockSpec((tb, D_MODEL), lambda i: (i, 0)),
                  pl.BlockSpec((1, 1, D_MODEL), lambda i: (i // blocks_per_batch, 0, 0)),
                  pl.BlockSpec(memory_space=pltpu.VMEM)],
        out_specs=pl.BlockSpec((tb, D_MODEL), lambda i: (i, 0)),
        out_shape=jax.ShapeDtypeStruct((t, D_MODEL), F32),
        scratch_shapes=[pltpu.VMEM((GATHER_SLOTS, ROWS_PER_EXPERT * GATHER_STRIDE, LANES), jnp.int32)],
        compiler_params=_cparams("arbitrary"),
        name="peer_v",
    )(*idx_parts, w, x, gate1, vtab)


def peer_layer(x, shift, scale, gate1, gain, w_q, keys, utab, vtab):
    bsz, s, d = x.shape
    t = bsz * s
    h, idx, gates = peer_router(x, shift, scale, gain, w_q, keys)
    idx = idx.reshape(t, IDX_SPLIT, N_SEL // IDX_SPLIT)
    idx_parts = [idx[:, a, :].reshape(-1) for a in range(IDX_SPLIT)]
    w = peer_u_phase(idx_parts, h.reshape(t, d), gates.reshape(t, N_SEL), utab)
    out = peer_v_phase(idx_parts, w, x.reshape(t, d), gate1, vtab)
    return out.reshape(bsz, s, d)


def _split3(x):
    a = x.astype(BF16)
    r = x - a.astype(F32)
    b = r.astype(BF16)
    c = (r - b.astype(F32)).astype(BF16)
    return a, b, c


def _kvq_kernel(x_ref, kshift_ref, kscale_ref, kgain_ref, qshift_ref, qscale_ref, qgain_ref,
                wkv_ref, wf_hi_ref, wf_lo_ref, fb_ref, wq_ref,
                q_ref, k_ref, v_ref, cf_ref, carry_ref, *, ts):
    d = D_MODEL
    x = x_ref[0]
    r = lax.rsqrt(jnp.mean(x * x, axis=-1, keepdims=True) + RMS_EPS)
    xn = x * r
    hk = xn * kgain_ref[...] * (1.0 + kscale_ref[0]) + kshift_ref[0]
    hq = xn * qgain_ref[...] * (1.0 + qscale_ref[0]) + qshift_ref[0]
    hk_hi = hk.astype(BF16)
    kv = jnp.dot(hk_hi, wkv_ref[...], preferred_element_type=F32)
    k_ref[0] = kv[:, :d].astype(BF16)
    v_ref[0] = kv[:, d:].astype(BF16)
    q = jnp.dot(hq.astype(BF16), wq_ref[...], preferred_element_type=F32)
    q_ref[0] = (q * (LOG2_E * HEAD_DIM ** -0.5)).astype(BF16)

    hk_lo = (hk - hk_hi.astype(F32)).astype(BF16)
    f_logit = (jnp.dot(hk_hi, wf_hi_ref[...], preferred_element_type=F32)
               + jnp.dot(hk_lo, wf_hi_ref[...], preferred_element_type=F32)
               + jnp.dot(hk_hi, wf_lo_ref[...], preferred_element_type=F32))
    z = f_logit + fb_ref[...]
    log_f = LOG2_E * (jnp.minimum(z, 0.0) - jnp.log1p(jnp.exp(-jnp.abs(z))))

    @pl.when(pl.program_id(1) == 0)
    def _():
        carry_ref[...] = jnp.zeros_like(carry_ref)

    tri = (lax.broadcasted_iota(jnp.int32, (ts, ts), 0)
           >= lax.broadcasted_iota(jnp.int32, (ts, ts), 1)).astype(BF16)
    cum = carry_ref[...]
    for piece in _split3(log_f):
        cum = cum + jnp.dot(tri, piece, preferred_element_type=F32)
    cf_ref[0] = cum
    carry_ref[...] = cum[ts - 1:ts, :]


def kvq_proj(x, kshift, kscale, kgain, qshift, qscale, qgain, w_kv, wf_hi, wf_lo, f_bias, w_q, *, ts=512):
    bsz, s, d = x.shape
    vec = pl.BlockSpec((1, 1, d), lambda b, i: (b, 0, 0))
    const = lambda a: pl.BlockSpec(a.shape, lambda b, i: (0,) * a.ndim)
    tok = lambda width: pl.BlockSpec((1, ts, width), lambda b, i: (b, i, 0))
    return pl.pallas_call(
        functools.partial(_kvq_kernel, ts=ts),
        grid=(bsz, s // ts),
        in_specs=[tok(d), vec, vec, const(kgain), vec, vec, const(qgain),
                  const(w_kv), const(wf_hi), const(wf_lo), const(f_bias), const(w_q)],
        out_specs=[tok(d), tok(d), tok(d), tok(LANES)],
        out_shape=[jax.ShapeDtypeStruct((bsz, s, d), BF16)] * 3
        + [jax.ShapeDtypeStruct((bsz, s, LANES), F32)],
        scratch_shapes=[pltpu.VMEM((1, LANES), F32)],
        compiler_params=_cparams("arbitrary", "arbitrary"),
        name="kvq_proj",
    )(x, kshift, kscale, kgain, qshift, qscale, qgain, w_kv, wf_hi, wf_lo, f_bias, w_q)


def _fox_kernel(q_ref, k_ref, v_ref, cfq_ref, cfk_ref, o_ref, q2_ref, cq_ref, m_ref, l_ref, acc_ref, *, tq):
    pair = pl.program_id(1)
    qi = pl.program_id(2)
    reps = tq // LANES
    lane = lax.broadcasted_iota(jnp.int32, (tq, LANES), 1)
    first_head = lane < HEAD_DIM
    q = q_ref[0]
    zq = jnp.zeros_like(q)
    q2_ref[0:tq, :] = jnp.where(first_head, q, zq)
    q2_ref[tq:2 * tq, :] = jnp.where(first_head, zq, q)
    cf = cfq_ref[0]
    for sub in range(2):
        col = jnp.sum(jnp.where(lane == 2 * pair + sub, cf, 0.0), axis=1, keepdims=True)
        cq_ref[sub * tq:(sub + 1) * tq, :] = jnp.broadcast_to(col, (tq, LANES))
    m_ref[...] = jnp.full_like(m_ref, -jnp.inf)
    l_ref[...] = jnp.zeros_like(l_ref)
    acc_ref[...] = jnp.zeros_like(acc_ref)
    below_diag = (lax.broadcasted_iota(jnp.int32, (tq, tq), 1)
                  <= lax.broadcasted_iota(jnp.int32, (tq, tq), 0))

    def block(j, masked):
        k0 = pl.multiple_of(j * tq, tq)
        kb = k_ref[0, pl.ds(k0, tq), :]
        vb = v_ref[0, pl.ds(k0, tq), :]
        s = lax.dot_general(q2_ref[...], kb, NT_DIMS, preferred_element_type=F32)
        ck = cfk_ref[0, 0, :, pl.ds(k0, tq)]
        ps, alphas = [], []
        for sub in range(2):
            rows = slice(sub * tq, (sub + 1) * tq)
            ss = s[rows] + pltpu.repeat(cq_ref[rows, :], reps, axis=1) - ck[sub:sub + 1, :]
            if masked:
                ss = jnp.where(below_diag, ss, -jnp.inf)
            m_prev = m_ref[rows, :]
            m_new = jnp.maximum(m_prev, jnp.max(ss, axis=1, keepdims=True))
            alpha = jnp.exp2(m_prev - m_new)
            p = jnp.exp2(ss - pltpu.repeat(m_new, reps, axis=1))
            l_ref[rows, :] = alpha * l_ref[rows, :] + jnp.sum(p, axis=1, keepdims=True)
            m_ref[rows, :] = m_new
            ps.append(p.astype(BF16))
            alphas.append(alpha)
        pv = jnp.dot(jnp.concatenate(ps, axis=0), vb, preferred_element_type=F32)
        acc_ref[...] = jnp.concatenate(alphas, axis=0) * acc_ref[...] + pv

    def full_block(j, carry):
        block(j, False)
        return carry

    lax.fori_loop(0, qi, full_block, 0)
    block(qi, True)
    o = jnp.where(first_head, acc_ref[0:tq, :] / l_ref[0:tq, :], acc_ref[tq:2 * tq, :] / l_ref[tq:2 * tq, :])
    o_ref[0] = o.astype(o_ref.dtype)


def fox_attention(q, k, v, cf_tok, cf_head, *, tq=512):
    bsz, s, d = q.shape
    n_pairs = N_HEADS // 2
    return pl.pallas_call(
        functools.partial(_fox_kernel, tq=tq),
        grid=(bsz, n_pairs, s // tq),
        in_specs=[pl.BlockSpec((1, tq, LANES), lambda b, p, i: (b, i, p)),
                  pl.BlockSpec((1, s, LANES), lambda b, p, i: (b, 0, p)),
                  pl.BlockSpec((1, s, LANES), lambda b, p, i: (b, 0, p)),
                  pl.BlockSpec((1, tq, LANES), lambda b, p, i: (b, i, 0)),
                  pl.BlockSpec((1, 1, 2, s), lambda b, p, i: (b, p, 0, 0))],
        out_specs=pl.BlockSpec((1, tq, LANES), lambda b, p, i: (b, i, p)),
        out_shape=jax.ShapeDtypeStruct((bsz, s, d), BF16),
        scratch_shapes=[pltpu.VMEM((2 * tq, LANES), BF16),
                        pltpu.VMEM((2 * tq, LANES), F32),
                        pltpu.VMEM((2 * tq, LANES), F32),
                        pltpu.VMEM((2 * tq, LANES), F32),
                        pltpu.VMEM((2 * tq, LANES), F32)],
        compiler_params=_cparams("arbitrary", "arbitrary", "arbitrary"),
        name="fox_attention",
    )(q, k, v, cf_tok, cf_head)


def _out_proj_kernel(a_ref, x_ref, gate1_ref, w_ref, o_ref):
    y = jnp.dot(a_ref[0], w_ref[...], preferred_element_type=F32)
    o_ref[0] = x_ref[0] + gate1_ref[0] * y


def out_proj(a, x, gate1, w, *, ts=512):
    bsz, s, d = x.shape
    tok = pl.BlockSpec((1, ts, d), lambda b, i: (b, i, 0))
    return pl.pallas_call(
        _out_proj_kernel,
        grid=(bsz, s // ts),
        in_specs=[tok, tok, pl.BlockSpec((1, 1, d), lambda b, i: (b, 0, 0)),
                  pl.BlockSpec((d, d), lambda b, i: (0, 0))],
        out_specs=tok,
        out_shape=jax.ShapeDtypeStruct((bsz, s, d), F32),
        compiler_params=_cparams("arbitrary", "arbitrary"),
        name="out_proj",
    )(a, x, gate1, w)


def _final_norm_kernel(x_ref, gain_ref, o_ref):
    x = x_ref[...]
    r = lax.rsqrt(jnp.mean(x * x, axis=-1, keepdims=True) + RMS_EPS)
    o_ref[...] = (x * r) * gain_ref[...]


def final_norm_call(x, gain, *, ts=1024):
    t, d = x.shape
    return pl.pallas_call(
        _final_norm_kernel,
        grid=(t // ts,),
        in_specs=[pl.BlockSpec((ts, d), lambda i: (i, 0)), pl.BlockSpec((1, d), lambda i: (0, 0))],
        out_specs=pl.BlockSpec((ts, d), lambda i: (i, 0)),
        out_shape=jax.ShapeDtypeStruct((t, d), F32),
        compiler_params=_cparams("arbitrary"),
        name="final_norm",
    )(x, gain)


def kernel(x, c, mix_norm, mix_ada_w, mix_ada_b, ffn_norm, ffn_ada_w, ffn_ada_b, a_w_in, a_conv, a_w_out, kv_norm, kv_ada_w, kv_ada_b, kv_w, f_bias, b_w_q, b_w_o, peer_w_q, peer_sub_keys, peer_u, peer_v, final_norm):
    bsz, s, d = x.shape
    depth = mix_norm.shape[0]
    n_a = a_w_in.shape[0]

    ada_w = jnp.concatenate([mix_ada_w[l] for l in range(depth)] + [ffn_ada_w[l] for l in range(depth)]
                            + [kv_ada_w], axis=1)
    ada_b = jnp.concatenate([mix_ada_b[l] for l in range(depth)] + [ffn_ada_b[l] for l in range(depth)]
                            + [kv_ada_b], axis=0)
    ada = ada_linear(c, ada_w, ada_b)

    def ada_chunk(i):
        return ada[:, i * d:(i + 1) * d].reshape(bsz, 1, d)

    def mix_ada(l):
        return ada_chunk(3 * l), ada_chunk(3 * l + 1), 1.0 + ada_chunk(3 * l + 2)

    def ffn_ada(l):
        o = 3 * depth + 3 * l
        return ada_chunk(o), ada_chunk(o + 1), 1.0 + ada_chunk(o + 2)

    kv_shift, kv_scale = ada_chunk(6 * depth), ada_chunk(6 * depth + 1)

    u_tabs = pack_table(peer_u)
    v_tabs = pack_table(peer_v)
    kv = None
    for layer in range(depth):
        shift, scale, gate1 = mix_ada(layer)
        gain = mix_norm[layer].reshape(1, d)
        if layer < n_a:
            x = mixer_a(x, shift, scale, gate1, gain, a_w_in[layer].astype(BF16), a_conv[layer],
                        a_w_out[layer].astype(BF16))
        else:
            j = layer - n_a
            if j == 0:
                w_f = jnp.pad(kv_w[:, 2 * d:], ((0, 0), (0, LANES - N_HEADS)))
                wf_hi = w_f.astype(BF16)
                wf_lo = (w_f - wf_hi.astype(F32)).astype(BF16)
                fb = jnp.pad(f_bias, (0, LANES - N_HEADS)).reshape(1, LANES)
                q, k, v, cf = kvq_proj(x, kv_shift, kv_scale, kv_norm.reshape(1, d), shift, scale, gain,
                                       kv_w[:, :2 * d].astype(BF16), wf_hi, wf_lo, fb, b_w_q[j].astype(BF16))
                kv = (k, v, cf, jnp.swapaxes(cf[:, :, :N_HEADS], 1, 2).reshape(bsz, N_HEADS // 2, 2, s))
            else:
                raise NotImplementedError("more than one attention layer")
            attn = fox_attention(q, kv[0], kv[1], kv[2], kv[3])
            x = out_proj(attn, x, gate1, b_w_o[j].astype(BF16))
        shift, scale, gate1 = ffn_ada(layer)
        keys = peer_sub_keys[layer].reshape(2 * PEER_HEADS, PEER_N_KEYS, -1).astype(BF16)
        x = peer_layer(x, shift, scale, gate1, ffn_norm[layer].reshape(1, d), peer_w_q[layer].astype(BF16),
                       keys, u_tabs[layer], v_tabs[layer])
    return final_norm_call(x.reshape(bsz * s, d), final_norm.reshape(1, d)).reshape(bsz, s, d)
```

```python
import functools

import jax
import jax.numpy as jnp
from jax import lax
from jax.experimental import pallas as pl
from jax.experimental.pallas import tpu as pltpu

D_MODEL = 1024
N_HEADS = 16
HEAD_DIM = 64
PEER_HEADS = 8
PEER_N_KEYS = 128
PEER_TOPK = 16
N_SEL = PEER_HEADS * PEER_TOPK
RMS_EPS = 1e-6
LOG2_E = 1.4426950408889634

LANES = 128
SUBLANES = 8
HALF_D = D_MODEL // 2
ROWS_PER_EXPERT = HALF_D // LANES
IDX_SPLIT = 8
SUBGROUPS = 16
GATHER_SLOTS = 32
GATHER_STRIDE = 136
VMEM_LIMIT = 56 * 1024 * 1024

BF16 = jnp.bfloat16
F32 = jnp.float32
NT_DIMS = (((1,), (1,)), ((), ()))


def _cparams(*sem):
    return pltpu.CompilerParams(dimension_semantics=sem, vmem_limit_bytes=VMEM_LIMIT)


def _norm_mod(x, gain, shift, scale):
    r = lax.rsqrt(jnp.mean(x * x, axis=-1, keepdims=True) + RMS_EPS)
    return (x * r) * gain * (1.0 + scale) + shift


def _ada_kernel(c_ref, w_ref, b_ref, o_ref):
    c = c_ref[...]
    sc = c * jax.nn.sigmoid(c)
    o_ref[...] = jnp.dot(sc, w_ref[...], precision=lax.Precision.HIGHEST,
                         preferred_element_type=F32) + b_ref[...]


def ada_linear(c, w, b, *, tn=1024):
    bsz, d = c.shape
    n = w.shape[1]
    return pl.pallas_call(
        _ada_kernel,
        grid=(n // tn,),
        in_specs=[pl.BlockSpec((bsz, d), lambda j: (0, 0)),
                  pl.BlockSpec((d, tn), lambda j: (0, j)),
                  pl.BlockSpec((1, tn), lambda j: (0, j))],
        out_specs=pl.BlockSpec((bsz, tn), lambda j: (0, j)),
        out_shape=jax.ShapeDtypeStruct((bsz, n), F32),
        compiler_params=_cparams("arbitrary"),
        name="ada_linear",
    )(c, w, b.reshape(1, n))


def _mixer_a_kernel(x_ref, shift_ref, scale_ref, gate1_ref, gain_ref, win_ref, conv_ref, wout_ref,
                    o_ref, cu_ref, *, ts):
    d = D_MODEL
    x = x_ref[0]
    h = _norm_mod(x, gain_ref[...], shift_ref[0], scale_ref[0])
    proj = jnp.dot(h.astype(BF16), win_ref[...], preferred_element_type=F32)
    b_gate = proj[:, :d]
    cu = proj[:, d:2 * d] * proj[:, 2 * d:]

    @pl.when(pl.program_id(1) == 0)
    def _():
        cu_ref[0:SUBLANES, :] = jnp.zeros((SUBLANES, d), F32)

    cu_ref[SUBLANES:SUBLANES + ts, :] = cu
    w = conv_ref[...]
    z = (w[0:1] * cu_ref[SUBLANES - 2:SUBLANES - 2 + ts, :]
         + w[1:2] * cu_ref[SUBLANES - 1:SUBLANES - 1 + ts, :]
         + w[2:3] * cu)
    cu_ref[0:SUBLANES, :] = cu_ref[ts:ts + SUBLANES, :]
    y = jnp.dot((b_gate * z).astype(BF16), wout_ref[...], preferred_element_type=F32)
    o_ref[0] = x + gate1_ref[0] * y


def mixer_a(x, shift, scale, gate1, gain, w_in, conv_w, w_out, *, ts=256):
    bsz, s, d = x.shape
    vec = pl.BlockSpec((1, 1, d), lambda b, i: (b, 0, 0))
    return pl.pallas_call(
        functools.partial(_mixer_a_kernel, ts=ts),
        grid=(bsz, s // ts),
        in_specs=[pl.BlockSpec((1, ts, d), lambda b, i: (b, i, 0)), vec, vec, vec,
                  pl.BlockSpec((1, d), lambda b, i: (0, 0)),
                  pl.BlockSpec((d, 3 * d), lambda b, i: (0, 0)),
                  pl.BlockSpec((3, d), lambda b, i: (0, 0)),
                  pl.BlockSpec((d, d), lambda b, i: (0, 0))],
        out_specs=pl.BlockSpec((1, ts, d), lambda b, i: (b, i, 0)),
        out_shape=jax.ShapeDtypeStruct((bsz, s, d), F32),
        scratch_shapes=[pltpu.VMEM((ts + 2 * SUBLANES, d), F32)],
        compiler_params=_cparams("arbitrary", "arbitrary"),
        name="mixer_a",
    )(x, shift, scale, gate1, gain, w_in, conv_w, w_out)


def _topk_rows(s, k, payload=None):
    n = s.shape[0]
    rows = lax.broadcasted_iota(jnp.int32, s.shape, 0).astype(F32)
    vals, idxs, pays = [], [], []
    for _ in range(k):
        m = jnp.max(s, axis=0, keepdims=True)
        am = jnp.min(jnp.where(s == m, rows, float(n)), axis=0, keepdims=True)
        hit = rows == am
        vals.append(m)
        idxs.append(am)
        if payload is not None:
            pays.append(jnp.sum(jnp.where(hit, payload, 0), axis=0, keepdims=True))
        s = jnp.where(hit, -jnp.inf, s)
    return vals, idxs, pays


def _topk_rows_paired(s, k):
    n = s.shape[0]
    half = n // 2
    lo, hi = s[:half], s[half:]
    rows = lax.broadcasted_iota(jnp.int32, lo.shape, 0).astype(F32)
    swap = hi > lo
    a = jnp.where(swap, hi, lo)
    b = jnp.where(swap, lo, hi)
    ia = jnp.where(swap, rows + half, rows)
    ib = jnp.where(swap, rows, rows + half)
    vals, idxs = [], []
    for _ in range(k):
        m = jnp.max(a, axis=0, keepdims=True)
        am = jnp.min(jnp.where(a == m, ia, float(n)), axis=0, keepdims=True)
        hit = ia == am
        vals.append(m)
        idxs.append(am)
        a = jnp.where(hit, b, a)
        ia = jnp.where(hit, ib, ia)
        b = jnp.where(hit, -jnp.inf, b)
    return vals, idxs


def _router_kernel(x_ref, shift_ref, scale_ref, gain_ref, wq_ref, keys_ref, h_ref, idx_ref, gates_ref):
    x = x_ref[0]
    h = _norm_mod(x, gain_ref[...], shift_ref[0], scale_ref[0])
    hb = h.astype(BF16)
    h_ref[0] = hb
    q = jnp.dot(hb, wq_ref[...], preferred_element_type=F32).astype(BF16)
    idx_rows, gate_rows = [], []
    for head in range(PEER_HEADS):
        top_s, top_i = [], []
        for part in range(2):
            hp = head * 2 + part
            st = lax.dot_general(keys_ref[hp], q[:, hp * LANES:(hp + 1) * LANES], NT_DIMS,
                                 preferred_element_type=F32)
            vals, idxs = _topk_rows_paired(st, PEER_TOPK)
            top_s.append(vals)
            top_i.append(idxs)
        s2 = jnp.concatenate(top_s[1], axis=0)
        i2 = jnp.concatenate(top_i[1], axis=0).astype(jnp.int32)
        n_b = [PEER_TOPK // (a + 1) for a in range(PEER_TOPK)]
        pad = -(-sum(n_b) // SUBLANES) * SUBLANES - sum(n_b)
        cand_s = jnp.concatenate([top_s[0][a] + s2[:n_b[a]] for a in range(PEER_TOPK)]
                                 + [jnp.full((pad, s2.shape[1]), -jnp.inf, F32)], axis=0)
        cand_i = jnp.concatenate([top_i[0][a].astype(jnp.int32) * PEER_N_KEYS + i2[:n_b[a]]
                                  for a in range(PEER_TOPK)]
                                 + [jnp.zeros((pad, s2.shape[1]), jnp.int32)], axis=0)
        best_s, _, best_e = _topk_rows(cand_s, PEER_TOPK, payload=cand_i)
        bs = jnp.concatenate(best_s, axis=0)
        e = jnp.exp(bs - bs[0:1])
        gate_rows.append(e / jnp.sum(e, axis=0, keepdims=True))
        idx_rows.append(jnp.concatenate(best_e, axis=0))
    idx_ref[0] = (jnp.concatenate(idx_rows, axis=0) * ROWS_PER_EXPERT).T
    gates_ref[0] = jnp.concatenate(gate_rows, axis=0).T


def peer_router(x, shift, scale, gain, w_q, keys, *, tb=256):
    bsz, s, d = x.shape
    vec = pl.BlockSpec((1, 1, d), lambda b, i: (b, 0, 0))
    tok = lambda width: pl.BlockSpec((1, tb, width), lambda b, i: (b, i, 0))
    return pl.pallas_call(
        _router_kernel,
        grid=(bsz, s // tb),
        in_specs=[tok(d), vec, vec,
                  pl.BlockSpec((1, d), lambda b, i: (0, 0)),
                  pl.BlockSpec(w_q.shape, lambda b, i: (0, 0)),
                  pl.BlockSpec(keys.shape, lambda b, i: (0, 0, 0))],
        out_specs=[tok(d), tok(N_SEL), tok(N_SEL)],
        out_shape=[jax.ShapeDtypeStruct((bsz, s, d), BF16),
                   jax.ShapeDtypeStruct((bsz, s, N_SEL), jnp.int32),
                   jax.ShapeDtypeStruct((bsz, s, N_SEL), F32)],
        compiler_params=_cparams("arbitrary", "arbitrary"),
        name="peer_router",
    )(x, shift, scale, gain, w_q, keys)


def pack_table(tab):
    bits = lax.bitcast_convert_type(tab, jnp.uint32)
    bf16_bits = (bits + jnp.uint32(0x7FFF) + ((bits >> 16) & jnp.uint32(1))) >> 16
    words = bf16_bits[..., :HALF_D] | (bf16_bits[..., HALF_D:] << 16)
    return lax.bitcast_convert_type(words, jnp.int32).reshape(*tab.shape[:-2], -1, LANES)


def _unpack(words):
    lo = pltpu.bitcast(words << 16, F32)
    hi = pltpu.bitcast(words & jnp.int32(-65536), F32)
    return lo, hi


def _gather_tile(idx_refs, tok, tab_ref, g_ref, slot):
    per = N_SEL // IDX_SPLIT
    base = tok * per
    for m in range(per):
        for a in range(IDX_SPLIT):
            k = a * per + m
            e4 = pl.multiple_of(idx_refs[a][base + m], ROWS_PER_EXPERT)
            g_ref[slot, pl.ds(k, ROWS_PER_EXPERT, stride=GATHER_STRIDE), :] = tab_ref[pl.ds(e4, ROWS_PER_EXPERT), :]


def _tile_halves(g_ref, slot, c):
    lo, hi = _unpack(g_ref[slot, pl.ds(c * GATHER_STRIDE, N_SEL), :])
    return lo.astype(BF16), hi.astype(BF16)


def _peer_u_kernel(*refs, tb):
    idx_refs = refs[:IDX_SPLIT]
    h_ref, gate_ref, tab_ref, w_ref, g_ref = refs[IDX_SPLIT:]

    def group(g, sub):
        r0 = pl.multiple_of((g * SUBGROUPS + sub) * SUBLANES, SUBLANES)
        x8 = h_ref[pl.ds(r0, SUBLANES), :]
        row = lax.broadcasted_iota(jnp.int32, (SUBLANES, N_SEL), 0)
        acc = jnp.zeros((SUBLANES, N_SEL), F32)
        for j in range(SUBLANES):
            slot = (sub * SUBLANES + j) % GATHER_SLOTS
            _gather_tile(idx_refs, r0 + j, tab_ref, g_ref, slot)
            res = jnp.zeros((SUBLANES, N_SEL), F32)
            for c in range(ROWS_PER_EXPERT):
                lo, hi = _tile_halves(g_ref, slot, c)
                rows = jnp.concatenate([lo, hi], axis=1)
                xs = jnp.concatenate([x8[:, c * LANES:(c + 1) * LANES],
                                      x8[:, HALF_D + c * LANES:HALF_D + (c + 1) * LANES]], axis=1)
                res += lax.dot_general(xs, rows, NT_DIMS, preferred_element_type=F32)
            acc = jnp.where(row == j, res, acc)
        w_ref[pl.ds(r0, SUBLANES), :] = gate_ref[pl.ds(r0, SUBLANES), :] * jax.nn.gelu(acc)

    def body(g, carry):
        for sub in range(SUBGROUPS):
            group(g, sub)
        return carry

    lax.fori_loop(0, tb // (SUBLANES * SUBGROUPS), body, 0)


def _idx_specs(tb):
    return [pl.BlockSpec((tb * (N_SEL // IDX_SPLIT),), lambda i: (i,), memory_space=pltpu.SMEM)] * IDX_SPLIT


def peer_u_phase(idx_parts, h, gates, utab, *, tb=SUBLANES * SUBGROUPS):
    t = h.shape[0]
    return pl.pallas_call(
        functools.partial(_peer_u_kernel, tb=tb),
        grid=(t // tb,),
        in_specs=_idx_specs(tb) + [
                  pl.BlockSpec((tb, D_MODEL), lambda i: (i, 0)),
                  pl.BlockSpec((tb, N_SEL), lambda i: (i, 0)),
                  pl.BlockSpec(memory_space=pltpu.VMEM)],
        out_specs=pl.BlockSpec((tb, N_SEL), lambda i: (i, 0)),
        out_shape=jax.ShapeDtypeStruct((t, N_SEL), F32),
        scratch_shapes=[pltpu.VMEM((GATHER_SLOTS, ROWS_PER_EXPERT * GATHER_STRIDE, LANES), jnp.int32)],
        compiler_params=_cparams("arbitrary"),
        name="peer_u",
    )(*idx_parts, h, gates, utab)


def _peer_v_kernel(*refs, tb, final):
    idx_refs = refs[:IDX_SPLIT]
    if final:
        w_ref, x_ref, gate1_ref, fgain_ref, tab_ref, o_ref, g_ref = refs[IDX_SPLIT:]
    else:
        w_ref, x_ref, gate1_ref, tab_ref, o_ref, g_ref = refs[IDX_SPLIT:]

    def group(g, sub):
        r0 = pl.multiple_of((g * SUBGROUPS + sub) * SUBLANES, SUBLANES)
        w8 = w_ref[pl.ds(r0, SUBLANES), :]
        w_hi = w8.astype(BF16)
        w_lo = (w8 - w_hi.astype(F32)).astype(BF16)
        row = lax.broadcasted_iota(jnp.int32, (SUBLANES, N_SEL), 0)
        zero = jnp.zeros_like(w_hi)
        y = jnp.zeros((2 * SUBLANES, D_MODEL), F32)
        for j in range(SUBLANES):
            slot = (sub * SUBLANES + j) % GATHER_SLOTS
            _gather_tile(idx_refs, r0 + j, tab_ref, g_ref, slot)
            wm = jnp.concatenate([jnp.where(row == j, w_hi, zero), jnp.where(row == j, w_lo, zero)], axis=0)
            halves = [_tile_halves(g_ref, slot, c) for c in range(ROWS_PER_EXPERT)]
            vals = jnp.concatenate([h[0] for h in halves] + [h[1] for h in halves], axis=1)
            y = y + jnp.dot(wm, vals, preferred_element_type=F32)
        out = x_ref[pl.ds(r0, SUBLANES), :] + gate1_ref[0] * (y[:SUBLANES] + y[SUBLANES:])
        if final:
            out = out * lax.rsqrt(jnp.mean(out * out, axis=-1, keepdims=True) + RMS_EPS) * fgain_ref[...]
        o_ref[pl.ds(r0, SUBLANES), :] = out

    def body(g, carry):
        for sub in range(SUBGROUPS):
            group(g, sub)
        return carry

    lax.fori_loop(0, tb // (SUBLANES * SUBGROUPS), body, 0)


def peer_v_phase(idx_parts, w, x, gate1, vtab, final_gain=None, *, tb=SUBLANES * SUBGROUPS):
    t = x.shape[0]
    blocks_per_batch = (t // gate1.shape[0]) // tb
    final = final_gain is not None
    extra_specs = [pl.BlockSpec((1, D_MODEL), lambda i: (0, 0))] if final else []
    extra_args = [final_gain] if final else []
    return pl.pallas_call(
        functools.partial(_peer_v_kernel, tb=tb, final=final),
        grid=(t // tb,),
        in_specs=_idx_specs(tb) + [
                  pl.BlockSpec((tb, N_SEL), lambda i: (i, 0)),
                  pl.BlockSpec((tb, D_MODEL), lambda i: (i, 0)),
                  pl.BlockSpec((1, 1, D_MODEL), lambda i: (i // blocks_per_batch, 0, 0))]
        + extra_specs + [pl.BlockSpec(memory_space=pltpu.VMEM)],
        out_specs=pl.BlockSpec((tb, D_MODEL), lambda i: (i, 0)),
        out_shape=jax.ShapeDtypeStruct((t, D_MODEL), F32),
        scratch_shapes=[pltpu.VMEM((GATHER_SLOTS, ROWS_PER_EXPERT * GATHER_STRIDE, LANES), jnp.int32)],
        compiler_params=_cparams("arbitrary"),
        name="peer_v",
    )(*idx_parts, w, x, gate1, *extra_args, vtab)


def peer_layer(x, shift, scale, gate1, gain, w_q, keys, utab, vtab, final_gain=None):
    bsz, s, d = x.shape
    t = bsz * s
    h, idx, gates = peer_router(x, shift, scale, gain, w_q, keys)
    idx = idx.reshape(t, IDX_SPLIT, N_SEL // IDX_SPLIT)
    idx_parts = [idx[:, a, :].reshape(-1) for a in range(IDX_SPLIT)]
    w = peer_u_phase(idx_parts, h.reshape(t, d), gates.reshape(t, N_SEL), utab)
    out = peer_v_phase(idx_parts, w, x.reshape(t, d), gate1, vtab, final_gain)
    return out.reshape(bsz, s, d)


def _split3(x):
    a = x.astype(BF16)
    r = x - a.astype(F32)
    b = r.astype(BF16)
    c = (r - b.astype(F32)).astype(BF16)
    return a, b, c


def _kvq_kernel(x_ref, kshift_ref, kscale_ref, kgain_ref, qshift_ref, qscale_ref, qgain_ref,
                wkv_ref, wf_hi_ref, wf_lo_ref, fb_ref, wq_ref,
                q_ref, k_ref, v_ref, cf_ref, carry_ref, *, ts):
    d = D_MODEL
    x = x_ref[0]
    r = lax.rsqrt(jnp.mean(x * x, axis=-1, keepdims=True) + RMS_EPS)
    xn = x * r
    hk = xn * kgain_ref[...] * (1.0 + kscale_ref[0]) + kshift_ref[0]
    hq = xn * qgain_ref[...] * (1.0 + qscale_ref[0]) + qshift_ref[0]
    hk_hi = hk.astype(BF16)
    kv = jnp.dot(hk_hi, wkv_ref[...], preferred_element_type=F32)
    k_ref[0] = kv[:, :d].astype(BF16)
    v_ref[0] = kv[:, d:].astype(BF16)
    q = jnp.dot(hq.astype(BF16), wq_ref[...], preferred_element_type=F32)
    q_ref[0] = (q * (LOG2_E * HEAD_DIM ** -0.5)).astype(BF16)

    hk_lo = (hk - hk_hi.astype(F32)).astype(BF16)
    f_logit = (jnp.dot(hk_hi, wf_hi_ref[...], preferred_element_type=F32)
               + jnp.dot(hk_lo, wf_hi_ref[...], preferred_element_type=F32)
               + jnp.dot(hk_hi, wf_lo_ref[...], preferred_element_type=F32))
    z = f_logit + fb_ref[...]
    log_f = LOG2_E * (jnp.minimum(z, 0.0) - jnp.log1p(jnp.exp(-jnp.abs(z))))

    @pl.when(pl.program_id(1) == 0)
    def _():
        carry_ref[...] = jnp.zeros_like(carry_ref)

    tri = (lax.broadcasted_iota(jnp.int32, (ts, ts), 0)
           >= lax.broadcasted_iota(jnp.int32, (ts, ts), 1)).astype(BF16)
    cum = carry_ref[...]
    for piece in _split3(log_f):
        cum = cum + jnp.dot(tri, piece, preferred_element_type=F32)
    cf_ref[0] = cum
    carry_ref[...] = cum[ts - 1:ts, :]


def kvq_proj(x, kshift, kscale, kgain, qshift, qscale, qgain, w_kv, wf_hi, wf_lo, f_bias, w_q, *, ts=512):
    bsz, s, d = x.shape
    vec = pl.BlockSpec((1, 1, d), lambda b, i: (b, 0, 0))
    const = lambda a: pl.BlockSpec(a.shape, lambda b, i: (0,) * a.ndim)
    tok = lambda width: pl.BlockSpec((1, ts, width), lambda b, i: (b, i, 0))
    return pl.pallas_call(
        functools.partial(_kvq_kernel, ts=ts),
        grid=(bsz, s // ts),
        in_specs=[tok(d), vec, vec, const(kgain), vec, vec, const(qgain),
                  const(w_kv), const(wf_hi), const(wf_lo), const(f_bias), const(w_q)],
        out_specs=[tok(d), tok(d), tok(d), tok(LANES)],
        out_shape=[jax.ShapeDtypeStruct((bsz, s, d), BF16)] * 3
        + [jax.ShapeDtypeStruct((bsz, s, LANES), F32)],
        scratch_shapes=[pltpu.VMEM((1, LANES), F32)],
        compiler_params=_cparams("arbitrary", "arbitrary"),
        name="kvq_proj",
    )(x, kshift, kscale, kgain, qshift, qscale, qgain, w_kv, wf_hi, wf_lo, f_bias, w_q)


def _fox_kernel(q_ref, k_ref, v_ref, cfq_ref, cfk_ref, o_ref, q2_ref, cq_ref, m_ref, l_ref, acc_ref, *, tq):
    pair = pl.program_id(1)
    qi = pl.program_id(2)
    reps = tq // LANES
    lane = lax.broadcasted_iota(jnp.int32, (tq, LANES), 1)
    first_head = lane < HEAD_DIM
    q = q_ref[0]
    zq = jnp.zeros_like(q)
    q2_ref[0:tq, :] = jnp.where(first_head, q, zq)
    q2_ref[tq:2 * tq, :] = jnp.where(first_head, zq, q)
    cf = cfq_ref[0]
    for sub in range(2):
        col = jnp.sum(jnp.where(lane == 2 * pair + sub, cf, 0.0), axis=1, keepdims=True)
        cq_ref[sub * tq:(sub + 1) * tq, :] = jnp.broadcast_to(col, (tq, LANES))
    m_ref[...] = jnp.full_like(m_ref, -jnp.inf)
    l_ref[...] = jnp.zeros_like(l_ref)
    acc_ref[...] = jnp.zeros_like(acc_ref)
    below_diag = (lax.broadcasted_iota(jnp.int32, (tq, tq), 1)
                  <= lax.broadcasted_iota(jnp.int32, (tq, tq), 0))

    def block(j, masked):
        k0 = pl.multiple_of(j * tq, tq)
        kb = k_ref[0, pl.ds(k0, tq), :]
        vb = v_ref[0, pl.ds(k0, tq), :]
        s = lax.dot_general(q2_ref[...], kb, NT_DIMS, preferred_element_type=F32)
        ck = cfk_ref[0, 0, :, pl.ds(k0, tq)]
        ps, alphas = [], []
        for sub in range(2):
            rows = slice(sub * tq, (sub + 1) * tq)
            ss = s[rows] + pltpu.repeat(cq_ref[rows, :], reps, axis=1) - ck[sub:sub + 1, :]
            if masked:
                ss = jnp.where(below_diag, ss, -jnp.inf)
            m_prev = m_ref[rows, :]
            m_new = jnp.maximum(m_prev, jnp.max(ss, axis=1, keepdims=True))
            alpha = jnp.exp2(m_prev - m_new)
            p = jnp.exp2(ss - pltpu.repeat(m_new, reps, axis=1))
            l_ref[rows, :] = alpha * l_ref[rows, :] + jnp.sum(p, axis=1, keepdims=True)
            m_ref[rows, :] = m_new
            ps.append(p.astype(BF16))
            alphas.append(alpha)
        pv = jnp.dot(jnp.concatenate(ps, axis=0), vb, preferred_element_type=F32)
        acc_ref[...] = jnp.concatenate(alphas, axis=0) * acc_ref[...] + pv

    def full_block(j, carry):
        block(j, False)
        return carry

    lax.fori_loop(0, qi, full_block, 0)
    block(qi, True)
    o = jnp.where(first_head, acc_ref[0:tq, :] / l_ref[0:tq, :], acc_ref[tq:2 * tq, :] / l_ref[tq:2 * tq, :])
    o_ref[0] = o.astype(o_ref.dtype)


def fox_attention(q, k, v, cf_tok, cf_head, *, tq=512):
    bsz, s, d = q.shape
    n_pairs = N_HEADS // 2
    return pl.pallas_call(
        functools.partial(_fox_kernel, tq=tq),
        grid=(bsz, n_pairs, s // tq),
        in_specs=[pl.BlockSpec((1, tq, LANES), lambda b, p, i: (b, i, p)),
                  pl.BlockSpec((1, s, LANES), lambda b, p, i: (b, 0, p)),
                  pl.BlockSpec((1, s, LANES), lambda b, p, i: (b, 0, p)),
                  pl.BlockSpec((1, tq, LANES), lambda b, p, i: (b, i, 0)),
                  pl.BlockSpec((1, 1, 2, s), lambda b, p, i: (b, p, 0, 0))],
        out_specs=pl.BlockSpec((1, tq, LANES), lambda b, p, i: (b, i, p)),
        out_shape=jax.ShapeDtypeStruct((bsz, s, d), BF16),
        scratch_shapes=[pltpu.VMEM((2 * tq, LANES), BF16),
                        pltpu.VMEM((2 * tq, LANES), F32),
                        pltpu.VMEM((2 * tq, LANES), F32),
                        pltpu.VMEM((2 * tq, LANES), F32),
                        pltpu.VMEM((2 * tq, LANES), F32)],
        compiler_params=_cparams("arbitrary", "arbitrary", "arbitrary"),
        name="fox_attention",
    )(q, k, v, cf_tok, cf_head)


def _out_proj_kernel(a_ref, x_ref, gate1_ref, w_ref, o_ref):
    y = jnp.dot(a_ref[0], w_ref[...], preferred_element_type=F32)
    o_ref[0] = x_ref[0] + gate1_ref[0] * y


def out_proj(a, x, gate1, w, *, ts=512):
    bsz, s, d = x.shape
    tok = pl.BlockSpec((1, ts, d), lambda b, i: (b, i, 0))
    return pl.pallas_call(
        _out_proj_kernel,
        grid=(bsz, s // ts),
        in_specs=[tok, tok, pl.BlockSpec((1, 1, d), lambda b, i: (b, 0, 0)),
                  pl.BlockSpec((d, d), lambda b, i: (0, 0))],
        out_specs=tok,
        out_shape=jax.ShapeDtypeStruct((bsz, s, d), F32),
        compiler_params=_cparams("arbitrary", "arbitrary"),
        name="out_proj",
    )(a, x, gate1, w)


def kernel(x, c, mix_norm, mix_ada_w, mix_ada_b, ffn_norm, ffn_ada_w, ffn_ada_b, a_w_in, a_conv, a_w_out, kv_norm, kv_ada_w, kv_ada_b, kv_w, f_bias, b_w_q, b_w_o, peer_w_q, peer_sub_keys, peer_u, peer_v, final_norm):
    bsz, s, d = x.shape
    depth = mix_norm.shape[0]
    n_a = a_w_in.shape[0]

    ada_w = jnp.concatenate([mix_ada_w[l] for l in range(depth)] + [ffn_ada_w[l] for l in range(depth)]
                            + [kv_ada_w], axis=1)
    ada_b = jnp.concatenate([mix_ada_b[l] for l in range(depth)] + [ffn_ada_b[l] for l in range(depth)]
                            + [kv_ada_b], axis=0)
    ada = ada_linear(c, ada_w, ada_b)

    def ada_chunk(i):
        return ada[:, i * d:(i + 1) * d].reshape(bsz, 1, d)

    def mix_ada(l):
        return ada_chunk(3 * l), ada_chunk(3 * l + 1), 1.0 + ada_chunk(3 * l + 2)

    def ffn_ada(l):
        o = 3 * depth + 3 * l
        return ada_chunk(o), ada_chunk(o + 1), 1.0 + ada_chunk(o + 2)

    kv_shift, kv_scale = ada_chunk(6 * depth), ada_chunk(6 * depth + 1)

    u_tabs = pack_table(peer_u)
    v_tabs = pack_table(peer_v)
    kv = None
    for layer in range(depth):
        shift, scale, gate1 = mix_ada(layer)
        gain = mix_norm[layer].reshape(1, d)
        if layer < n_a:
            x = mixer_a(x, shift, scale, gate1, gain, a_w_in[layer].astype(BF16), a_conv[layer],
                        a_w_out[layer].astype(BF16))
        else:
            j = layer - n_a
            if j == 0:
                w_f = jnp.pad(kv_w[:, 2 * d:], ((0, 0), (0, LANES - N_HEADS)))
                wf_hi = w_f.astype(BF16)
                wf_lo = (w_f - wf_hi.astype(F32)).astype(BF16)
                fb = jnp.pad(f_bias, (0, LANES - N_HEADS)).reshape(1, LANES)
                q, k, v, cf = kvq_proj(x, kv_shift, kv_scale, kv_norm.reshape(1, d), shift, scale, gain,
                                       kv_w[:, :2 * d].astype(BF16), wf_hi, wf_lo, fb, b_w_q[j].astype(BF16))
                kv = (k, v, cf, jnp.swapaxes(cf[:, :, :N_HEADS], 1, 2).reshape(bsz, N_HEADS // 2, 2, s))
            else:
                raise NotImplementedError("more than one attention layer")
            attn = fox_attention(q, kv[0], kv[1], kv[2], kv[3])
            x = out_proj(attn, x, gate1, b_w_o[j].astype(BF16))
        shift, scale, gate1 = ffn_ada(layer)
        keys = peer_sub_keys[layer].reshape(2 * PEER_HEADS, PEER_N_KEYS, -1).astype(BF16)
        x = peer_layer(x, shift, scale, gate1, ffn_norm[layer].reshape(1, d), peer_w_q[layer].astype(BF16),
                       keys, u_tabs[layer], v_tabs[layer],
                       final_norm.reshape(1, d) if layer == depth - 1 else None)
    return x
```
